```python
import math
import jax, jax.numpy as jnp
from jax import lax
import numpy as np

D_MODEL = 2048
BATCH = 4
SEQ = 2048
DEPTH = 4

GRID_W = 64
CTX_LEN = 256
MIX_W = 1024
N_BRANCH = 3
ML_HEADS = 4
ML_DH = MIX_W // ML_HEADS
ML_CHUNK = 64
ML_CONV = 3
S5_GROUP = 16
S5_GROUPS = MIX_W // S5_GROUP
S5_STATE = 64
DA_HEADS = 8
DA_DQK = 64
DA_DV = 2 * DA_DQK
DA_QBLOCK = 128
ROPE_BASE = 10000.0
N_EXPERTS = 16
N_EXPERT_GROUPS = 4
EXPERTS_PER_GROUP = N_EXPERTS // N_EXPERT_GROUPS
TOP_K = 2
D_FF_EXPERT = 512
EPS = 1e-6

IN_SPLIT_SIZES = (MIX_W, MIX_W, MIX_W, MIX_W, 4 * ML_HEADS,
                  MIX_W,
                  DA_HEADS * 2 * DA_DQK, DA_HEADS * 2 * DA_DQK, DA_HEADS * DA_DV,
                  N_BRANCH * D_MODEL)
IN_COLS = sum(IN_SPLIT_SIZES)
IN_SPLIT_POINTS = tuple(sum(IN_SPLIT_SIZES[:j + 1]) for j in range(len(IN_SPLIT_SIZES) - 1))

kernel_name = "hybrid_mlstm_s5_diffattn_moe_dit"

F32 = jnp.float32


def rmsnorm(x, g):
    x32 = x.astype(F32)
    y = x32 * lax.rsqrt(jnp.mean(x32 * x32, -1, keepdims=True) + EPS)
    return (y * g.astype(F32)).astype(x.dtype)


def conv_centred(x, w, b):
    K = w.shape[0]
    L = x.shape[1]
    pad = K // 2
    xp = jnp.pad(x, ((0, 0), (pad, pad), (0, 0)))
    y = b
    for j in range(K):
        y = y + xp[:, j:j + L] * w[j]
    return y


def mlstm_chunks(q, k, v, ig, lf, state):
    Bsz, H, L, dh = q.shape
    nc = L // ML_CHUNK

    def to_chunks(a):
        a = a.reshape(a.shape[:2] + (nc, ML_CHUNK) + a.shape[3:])
        return jnp.moveaxis(a, 2, 0)

    tril = jnp.tril(jnp.ones((ML_CHUNK, ML_CHUNK), bool))

    def step(carry, inp):
        C, n, m = carry
        qc, kc, vc, ic, fc = inp
        b = jnp.cumsum(fc, -1)
        log_inter = b + m[..., None]
        log_intra = jnp.where(tril, b[..., :, None] - b[..., None, :] + ic[..., None, :], -jnp.inf)
        m_t = jnp.maximum(log_inter, log_intra.max(-1))
        w_inter = jnp.exp(log_inter - m_t)
        s = jnp.einsum('bhtd,bhsd->bhts', qc, kc) * jnp.exp(log_intra - m_t[..., None])
        num = w_inter[..., None] * jnp.einsum('bhed,bhtd->bhte', C, qc) + jnp.einsum('bhts,bhse->bhte', s, vc)
        den = w_inter * jnp.einsum('bhd,bhtd->bht', n, qc) + s.sum(-1)
        h = num / jnp.maximum(jnp.abs(den), jnp.exp(-m_t))[..., None]
        b_last = b[..., -1]
        log_w = b_last[..., None] - b + ic
        m_new = jnp.maximum(b_last + m, log_w.max(-1))
        decay = jnp.exp(b_last + m - m_new)
        w_s = jnp.exp(log_w - m_new[..., None])
        C_new = decay[..., None, None] * C + jnp.einsum('bhse,bhsd->bhed', vc * w_s[..., None], kc)
        n_new = decay[..., None] * n + jnp.einsum('bhs,bhsd->bhd', w_s, kc)
        return (C_new, n_new, m_new), h

    state, h = lax.scan(step, state, tuple(to_chunks(a) for a in (q, k, v, ig, lf)))
    h = jnp.moveaxis(h, 0, 2).reshape(Bsz, H, L, dh)
    return h, state


def mlstm_branch(qa, ka, va, oa, gates, conv_w, conv_b, norm_g, n_ctx):
    Bsz, N, _ = qa.shape
    qk = jnp.concatenate([qa, ka], -1)
    qk = jax.nn.silu(jnp.concatenate([conv_centred(qk[:, :n_ctx], conv_w, conv_b),
                                      conv_centred(qk[:, n_ctx:], conv_w, conv_b)], 1))

    def heads(a):
        return a.reshape(Bsz, N, ML_HEADS, ML_DH).transpose(0, 2, 1, 3).astype(F32)

    q = heads(qk[..., :MIX_W])
    k = heads(qk[..., MIX_W:]) * (ML_DH ** -0.5)
    v = heads(va)
    g = gates.astype(F32).reshape(Bsz, N, 4, ML_HEADS).transpose(2, 0, 3, 1)
    zero = (jnp.zeros((Bsz, ML_HEADS, ML_DH, ML_DH), F32),
            jnp.zeros((Bsz, ML_HEADS, ML_DH), F32),
            jnp.zeros((Bsz, ML_HEADS), F32))
    outs = []
    for d in range(2):
        ig, lf = g[2 * d], jax.nn.log_sigmoid(g[2 * d + 1])
        rev = (lambda a: jnp.flip(a, 2)) if d == 1 else (lambda a: a)
        seqs = (q, k, v, ig, lf)
        h_ctx, st = mlstm_chunks(*[rev(a[:, :, :n_ctx]) for a in seqs], zero)
        h_lat, _ = mlstm_chunks(*[rev(a[:, :, n_ctx:]) for a in seqs], st)
        outs.append(jnp.concatenate([rev(h_ctx), rev(h_lat)], 2))
    h = (outs[0] + outs[1]).transpose(0, 2, 1, 3)
    h = rmsnorm(h, norm_g.reshape(ML_HEADS, ML_DH)).reshape(Bsz, N, MIX_W)
    return (jax.nn.sigmoid(oa.astype(F32)) * h).astype(qa.dtype)


def s5_discretize(lam_re, lam_im, log_dt, b_re, b_im):
    lam_re, lam_im = lam_re.astype(F32), lam_im.astype(F32)
    b_re, b_im = b_re.astype(F32), b_im.astype(F32)
    dt = jnp.exp(log_dt.astype(F32))[:, None]
    mag = jnp.exp(lam_re * dt)
    a_re, a_im = mag * jnp.cos(lam_im * dt), mag * jnp.sin(lam_im * dt)
    nr, ni = a_re - 1.0, a_im
    den = lam_re * lam_re + lam_im * lam_im
    fr = (nr * lam_re + ni * lam_im) / den
    fi = (ni * lam_re - nr * lam_im) / den
    bb_re = fr[..., None] * b_re - fi[..., None] * b_im
    bb_im = fr[..., None] * b_im + fi[..., None] * b_re
    return (a_re, a_im), (bb_re, bb_im)


def s5_scan_out(u, a, bb, c_re, c_im, s0):
    a_re, a_im = a
    bu_re = jnp.einsum('blgi,gpi->blgp', u, bb[0])
    bu_im = jnp.einsum('blgi,gpi->blgp', u, bb[1])
    bu_re = bu_re.at[:, 0].add(a_re * s0[0] - a_im * s0[1])
    bu_im = bu_im.at[:, 0].add(a_re * s0[1] + a_im * s0[0])

    def combine(e1, e2):
        ar1, ai1, br1, bi1 = e1
        ar2, ai2, br2, bi2 = e2
        return (ar2 * ar1 - ai2 * ai1, ar2 * ai1 + ai2 * ar1,
                ar2 * br1 - ai2 * bi1 + br2, ar2 * bi1 + ai2 * br1 + bi2)

    elems = (jnp.broadcast_to(a_re, bu_re.shape), jnp.broadcast_to(a_im, bu_im.shape), bu_re, bu_im)
    _, _, s_re, s_im = lax.associative_scan(combine, elems, axis=1)
    y = (jnp.einsum('gip,blgp->blgi', c_re.astype(F32), s_re)
         - jnp.einsum('gip,blgp->blgi', c_im.astype(F32), s_im))
    return y, (s_re[:, -1], s_im[:, -1])


def s5_branch(u, lam_re, lam_im, log_dt, b_re, b_im, c_re, c_im, d_skip, glu_w, glu_b, n_ctx):
    Bsz, N, W = u.shape
    u32 = u.astype(F32)
    ug = u32.reshape(Bsz, N, S5_GROUPS, S5_GROUP)
    zero = (jnp.zeros((Bsz, S5_GROUPS, S5_STATE), F32), jnp.zeros((Bsz, S5_GROUPS, S5_STATE), F32))
    y = d_skip.astype(F32) * u32
    for d in range(2):
        a, bb = s5_discretize(lam_re[d], lam_im[d], log_dt[d], b_re[d], b_im[d])
        rev = (lambda t: jnp.flip(t, 1)) if d == 1 else (lambda t: t)
        y_ctx, st = s5_scan_out(rev(ug[:, :n_ctx]), a, bb, c_re, c_im, zero)
        y_lat, _ = s5_scan_out(rev(ug[:, n_ctx:]), a, bb, c_re, c_im, st)
        y = y + jnp.concatenate([rev(y_ctx), rev(y_lat)], 1).reshape(Bsz, N, W)
    z = jax.nn.gelu(y)
    return (z * jax.nn.sigmoid(z @ glu_w.astype(F32) + glu_b.astype(F32))).astype(u.dtype)


def rope_1d(x, pos):
    half = x.shape[-1] // 2
    freqs = ROPE_BASE ** (-jnp.arange(half, dtype=F32) / half)
    ang = pos.astype(F32)[:, None] * freqs
    cos = jnp.cos(ang)[:, None, None, :]
    sin = jnp.sin(ang)[:, None, None, :]
    x32 = x.astype(F32)
    x1, x2 = x32[..., :half], x32[..., half:]
    return jnp.concatenate([x1 * cos - x2 * sin, x1 * sin + x2 * cos], -1).astype(x.dtype)


def rope_2d(x, rows, cols):
    half = x.shape[-1] // 2
    return jnp.concatenate([rope_1d(x[..., :half], rows), rope_1d(x[..., half:], cols)], -1)


def diff_attn(q, k, v, lam):
    s = jnp.einsum('bhqcd,bhkcd->cbhqk', q, k).astype(F32) * (DA_DQK ** -0.5)
    p = jax.nn.softmax(s, -1)
    a = (p[0] - lam * p[1]).astype(v.dtype)
    return jnp.einsum('bhqk,bhkd->bhqd', a, v)


def diff_branch(qd, kd, vd, qn_g, kn_g, lq1, lk1, lq2, lk2, sub_g, lam_init, n_ctx, lat_only):
    Bsz, N, _ = qd.shape
    L = N - n_ctx
    rows_n = L // GRID_W
    rows = jnp.repeat(jnp.arange(rows_n), GRID_W)
    cols = jnp.tile(jnp.arange(GRID_W), rows_n)
    q = rmsnorm(qd.reshape(Bsz, N, DA_HEADS, 2, DA_DQK), qn_g)
    k = rmsnorm(kd.reshape(Bsz, N, DA_HEADS, 2, DA_DQK), kn_g)
    q = jnp.concatenate([q[:, :n_ctx], rope_2d(q[:, n_ctx:], rows, cols)], 1).transpose(0, 2, 1, 3, 4)
    k = jnp.concatenate([k[:, :n_ctx], rope_2d(k[:, n_ctx:], rows, cols)], 1).transpose(0, 2, 1, 3, 4)
    v = vd.reshape(Bsz, N, DA_HEADS, DA_DV).transpose(0, 2, 1, 3)
    lam = (jnp.exp(jnp.sum(lq1.astype(F32) * lk1.astype(F32)))
           - jnp.exp(jnp.sum(lq2.astype(F32) * lk2.astype(F32))) + lam_init)
    nb = L // DA_QBLOCK
    qb = jnp.moveaxis(q[:, :, n_ctx:].reshape(Bsz, DA_HEADS, nb, DA_QBLOCK, 2, DA_DQK), 2, 0)
    o = lax.map(lambda qq: diff_attn(qq, k, v, lam), qb)
    o = jnp.moveaxis(o, 0, 2).reshape(Bsz, DA_HEADS, L, DA_DV)
    if not lat_only:
        o_ctx = diff_attn(q[:, :, :n_ctx], k[:, :, :n_ctx], v[:, :, :n_ctx], lam)
        o = jnp.concatenate([o_ctx, o], 2)
    o = rmsnorm(o, sub_g) * (1.0 - lam_init)
    return o.transpose(0, 2, 1, 3).reshape(Bsz, o.shape[2], DA_HEADS * DA_DV)


def moe(h, router_w, router_b, w_gate, w_up, w_down):
    Bsz, n, _ = h.shape
    score = jax.nn.sigmoid(jnp.einsum('bnd,de->bne', h, router_w).astype(F32))
    sel = score + router_b.astype(F32)
    gscore = lax.top_k(sel.reshape(Bsz, n, N_EXPERT_GROUPS, EXPERTS_PER_GROUP), 2)[0].sum(-1)
    gidx = jnp.argmax(gscore, -1)
    in_group = (jnp.arange(N_EXPERTS) // EXPERTS_PER_GROUP) == gidx[..., None]
    _, eidx = lax.top_k(jnp.where(in_group, sel, -jnp.inf), TOP_K)
    w = jnp.take_along_axis(score, eidx, -1)
    w = w / w.sum(-1, keepdims=True)
    gates = jnp.einsum('bnke,bnk->bne', jax.nn.one_hot(eidx, N_EXPERTS, dtype=F32), w).astype(h.dtype)

    def per_sample(args):
        hs, gs = args
        act = jax.nn.silu(jnp.einsum('nd,edf->nef', hs, w_gate)) * jnp.einsum('nd,edf->nef', hs, w_up)
        return jnp.einsum('nef,efd->nd', act * gs[..., None], w_down)

    return lax.map(per_sample, (h, gates))


def per_token(p_ctx, p_lat, n_ctx, n_lat, with_ctx):
    Bsz, d = p_lat.shape
    lat = jnp.broadcast_to(p_lat[:, None, :], (Bsz, n_lat, d))
    if not with_ctx:
        return lat
    return jnp.concatenate([jnp.broadcast_to(p_ctx, (Bsz, n_ctx, d)), lat], 1)


def setup_inputs(seed: int = 0) -> dict:
    key = jax.random.key(seed)
    ks = iter(jax.random.split(key, 48))

    def nrm(shape, s):
        return jax.random.normal(next(ks), shape, F32) * s

    def gain(shape):
        return 1.0 + nrm(shape, 0.02)

    D = D_MODEL
    G, P, I = S5_GROUPS, S5_STATE, S5_GROUP
    gates_off = 4 * MIX_W
    f_bias = jnp.linspace(3.0, 6.0, ML_HEADS, dtype=F32)
    b_in = nrm((DEPTH, IN_COLS), 0.02)
    b_in = b_in.at[:, gates_off + ML_HEADS:gates_off + 2 * ML_HEADS].add(f_bias)
    b_in = b_in.at[:, gates_off + 3 * ML_HEADS:gates_off + 4 * ML_HEADS].add(f_bias)
    return {
        "x": nrm((BATCH, SEQ, D), 1.0),
        "c": nrm((BATCH, D), 1.0),
        "ctx": nrm((BATCH, CTX_LEN, D), 1.0),
        "c_ctx": nrm((D,), 1.0),
        "norm1_g": gain((DEPTH, D)),
        "norm2_g": gain((DEPTH, D)),
        "ada_w": nrm((DEPTH, D, 6 * D), 0.5 * D ** -0.5),
        "ada_b": nrm((DEPTH, 6 * D), 0.02),
        "w_in": nrm((DEPTH, D, IN_COLS), D ** -0.5),
        "b_in": b_in,
        "ml_conv_w": nrm((DEPTH, ML_CONV, 2 * MIX_W), 0.5),
        "ml_conv_b": nrm((DEPTH, 2 * MIX_W), 0.02),
        "ml_norm_g": gain((DEPTH, MIX_W)),
        "s5_lam_re": -0.5 + nrm((DEPTH, 2, G, P), 0.01),
        "s5_lam_im": math.pi * jnp.arange(P, dtype=F32) + nrm((DEPTH, 2, G, P), 0.01),
        "s5_log_dt": jax.random.uniform(next(ks), (DEPTH, 2, G), F32, math.log(1e-3), math.log(1e-1)),
        "s5_b_re": nrm((DEPTH, 2, G, P, I), I ** -0.5),
        "s5_b_im": nrm((DEPTH, 2, G, P, I), I ** -0.5),
        "s5_c_re": nrm((DEPTH, G, I, P), 1.0),
        "s5_c_im": nrm((DEPTH, G, I, P), 1.0),
        "s5_d": nrm((DEPTH, MIX_W), 1.0),
        "s5_glu_w": nrm((DEPTH, MIX_W, MIX_W), MIX_W ** -0.5),
        "s5_glu_b": nrm((DEPTH, MIX_W), 0.02),
        "da_q_norm_g": gain((DEPTH, DA_DQK)),
        "da_k_norm_g": gain((DEPTH, DA_DQK)),
        "da_lam_q1": nrm((DEPTH, DA_DQK), 0.1),
        "da_lam_k1": nrm((DEPTH, DA_DQK), 0.1),
        "da_lam_q2": nrm((DEPTH, DA_DQK), 0.1),
        "da_lam_k2": nrm((DEPTH, DA_DQK), 0.1),
        "da_sub_norm_g": gain((DEPTH, DA_DV)),
        "branch_proj": nrm((DEPTH, N_BRANCH, MIX_W, D), MIX_W ** -0.5),
        "w_out": nrm((DEPTH, D, D), D ** -0.5),
        "router_w": nrm((D, N_EXPERTS), D ** -0.5),
        "router_b": nrm((N_EXPERTS,), 0.01),
        "exp_w_gate": nrm((DEPTH, N_EXPERTS, D, D_FF_EXPERT), D ** -0.5),
        "exp_w_up": nrm((DEPTH, N_EXPERTS, D, D_FF_EXPERT), D ** -0.5),
        "exp_w_down": nrm((DEPTH, N_EXPERTS, D_FF_EXPERT, D), D_FF_EXPERT ** -0.5),
    }


def reference(x, c, ctx, c_ctx, norm1_g, norm2_g, ada_w, ada_b, w_in, b_in, ml_conv_w, ml_conv_b,
              ml_norm_g, s5_lam_re, s5_lam_im, s5_log_dt, s5_b_re, s5_b_im, s5_c_re, s5_c_im, s5_d,
              s5_glu_w, s5_glu_b, da_q_norm_g, da_k_norm_g, da_lam_q1, da_lam_k1, da_lam_q2, da_lam_k2,
              da_sub_norm_g, branch_proj, w_out, router_w, router_b, exp_w_gate, exp_w_up, exp_w_down):
    n_ctx = ctx.shape[1]
    n_lat = x.shape[1]
    z = jnp.concatenate([ctx, x], 1)
    for i in range(DEPTH):
        last = i == DEPTH - 1
        lam_init = 0.8 - 0.6 * math.exp(-0.3 * i)
        mod_lat = jnp.split(jax.nn.silu(c) @ ada_w[i] + ada_b[i], 6, -1)
        mod_ctx = jnp.split(jax.nn.silu(c_ctx) @ ada_w[i] + ada_b[i], 6, -1)

        h = (rmsnorm(z, norm1_g[i]) * (1.0 + per_token(mod_ctx[1], mod_lat[1], n_ctx, n_lat, True))
             + per_token(mod_ctx[0], mod_lat[0], n_ctx, n_lat, True))
        proj = h @ w_in[i] + b_in[i]
        qa, ka, va, oa, ga, us, qd, kd, vd, gb = jnp.split(proj, IN_SPLIT_POINTS, -1)
        y_a = mlstm_branch(qa, ka, va, oa, ga, ml_conv_w[i], ml_conv_b[i], ml_norm_g[i], n_ctx)
        y_b = s5_branch(us, s5_lam_re[i], s5_lam_im[i], s5_log_dt[i], s5_b_re[i], s5_b_im[i],
                        s5_c_re[i], s5_c_im[i], s5_d[i], s5_glu_w[i], s5_glu_b[i], n_ctx)
        y_c = diff_branch(qd, kd, vd, da_q_norm_g[i], da_k_norm_g[i], da_lam_q1[i], da_lam_k1[i],
                          da_lam_q2[i], da_lam_k2[i], da_sub_norm_g[i], lam_init, n_ctx, last)
        if last:
            y_a, y_b, gb, z = y_a[:, n_ctx:], y_b[:, n_ctx:], gb[:, n_ctx:], z[:, n_ctx:]
        ys = jnp.stack([y_a, y_b, y_c], 2)
        gates = jax.nn.sigmoid(gb.reshape(gb.shape[:2] + (N_BRANCH, D_MODEL)))
        merged = jnp.einsum('bnrd,bnrd->bnd', jnp.einsum('bnrw,rwd->bnrd', ys, branch_proj[i]), gates)
        z = z + per_token(mod_ctx[2], mod_lat[2], n_ctx, n_lat, not last) * (merged @ w_out[i])

        h2 = (rmsnorm(z, norm2_g[i]) * (1.0 + per_token(mod_ctx[4], mod_lat[4], n_ctx, n_lat, not last))
              + per_token(mod_ctx[3], mod_lat[3], n_ctx, n_lat, not last))
        z = z + per_token(mod_ctx[5], mod_lat[5], n_ctx, n_lat, not last) * moe(
            h2, router_w, router_b, exp_w_gate[i], exp_w_up[i], exp_w_down[i])
    return z
```

```python
import functools
import math

import jax
import jax.numpy as jnp
from jax import lax
from jax.experimental import pallas as pl
from jax.experimental.pallas import tpu as pltpu

F32 = jnp.float32
BF16 = jnp.bfloat16

D_MODEL = 2048
GRID_W = 64
MIX_W = 1024
N_BRANCH = 3
ML_HEADS = 4
ML_DH = MIX_W // ML_HEADS
S5_GROUP = 16
S5_GROUPS = MIX_W // S5_GROUP
S5_STATE = 64
S5_BUNDLE = 16
S5_NB = S5_GROUPS // S5_BUNDLE
S5_NS = S5_GROUPS * S5_STATE
DA_HEADS = 8
DA_DQK = 64
DA_DV = 2 * DA_DQK
ROPE_BASE = 10000.0
N_EXPERTS = 16
N_EXPERT_GROUPS = 4
EXPERTS_PER_GROUP = N_EXPERTS // N_EXPERT_GROUPS
D_FF_EXPERT = 512
EPS = 1e-6

COL_QA, COL_KA, COL_VA, COL_OA, COL_US, COL_QD, COL_KD, COL_VD, COL_GB = (
    0, 1024, 2048, 3072, 4096, 5120, 6144, 7168, 8192)
MAIN_COLS = 8192 + N_BRANCH * D_MODEL
GATE_OFF = 4 * MIX_W
N_GATES = 4 * ML_HEADS

SEQ_CHUNK = 256
LANES = 128
SUBLANES = 8
MIB = 1024 * 1024


def _params(sem, vmem_mib):
    return pltpu.CompilerParams(dimension_semantics=sem, vmem_limit_bytes=vmem_mib * MIB)


def _dot(a, b):
    return jnp.dot(a, b, preferred_element_type=F32)


def _dot_nt(a, b):
    return lax.dot_general(a, b, (((1,), (1,)), ((), ())), preferred_element_type=F32)


def _split3(x):
    hi = x.astype(BF16)
    r = x - hi.astype(F32)
    mid = r.astype(BF16)
    lo = (r - mid.astype(F32)).astype(BF16)
    return hi, mid, lo


def _rev_chunk(j, n_ctx_chunks, n_chunks):
    return jnp.where(j < n_ctx_chunks, n_ctx_chunks - 1 - j, n_chunks - 1 - (j - n_ctx_chunks))


def _ada_kernel(c_ref, w_ref, b_ref, o_ref):
    c = c_ref[...]
    s = (c * jax.nn.sigmoid(c)).astype(BF16)
    o_ref[0] = _dot(s, w_ref[0].astype(BF16)) + b_ref[0]


def ada_mod(cvec, ada_w, ada_b):
    depth, d, n6 = ada_w.shape
    rows = cvec.shape[0]
    tn = 1024
    return pl.pallas_call(
        _ada_kernel,
        grid=(depth, n6 // tn),
        in_specs=[pl.BlockSpec((rows, d), lambda l, j: (0, 0)),
                  pl.BlockSpec((1, d, tn), lambda l, j: (l, 0, j)),
                  pl.BlockSpec((1, 1, tn), lambda l, j: (l, 0, j))],
        out_specs=pl.BlockSpec((1, rows, tn), lambda l, j: (l, 0, j)),
        out_shape=jax.ShapeDtypeStruct((depth, rows, n6), F32),
        compiler_params=_params(("arbitrary", "arbitrary"), 40),
        name="ada_mod",
    )(cvec, ada_w, ada_b.reshape(depth, 1, n6))


def _mod_spec(layer, chunk, batch, ctx_tiles):
    return pl.BlockSpec((None, None, 1, D_MODEL),
                        lambda b, t: (layer, jnp.where(t < ctx_tiles, batch, b), 0, chunk))


def _norm_mod_kernel(*refs, has_delta, n_out):
    if has_delta:
        z_ref, d_ref, gate_ref, g_ref, shift_ref, scale_ref = refs[:6]
        outs = refs[6:]
        z = z_ref[0] + gate_ref[...] * d_ref[0]
        outs[0][0] = z
        outs = outs[1:]
    else:
        z_ref, g_ref, shift_ref, scale_ref = refs[:4]
        outs = refs[4:]
        z = z_ref[0]
    y = z * lax.rsqrt(jnp.mean(z * z, axis=-1, keepdims=True) + EPS) * g_ref[...]
    h = y * (1.0 + scale_ref[...]) + shift_ref[...]
    for o in outs[:n_out]:
        o[0] = h.astype(o.dtype)


def norm_mod(z, norm_g, mods, layer, shift_chunk, scale_chunk, n_ctx, out_dtypes,
             delta=None, gate_layer=None, gate_chunk=None):
    bsz, n, d = z.shape
    tm = SEQ_CHUNK
    ctx_tiles = n_ctx // tm
    tile = pl.BlockSpec((1, tm, d), lambda b, t: (b, t, 0))
    g_spec = pl.BlockSpec((None, 1, d), lambda b, t: (layer, 0, 0))
    args, specs = [z], [tile]
    out_shape, out_specs = [], []
    if delta is not None:
        args += [delta, mods]
        specs += [tile, _mod_spec(gate_layer, gate_chunk, bsz, ctx_tiles)]
        out_shape.append(jax.ShapeDtypeStruct(z.shape, F32))
        out_specs.append(tile)
    args += [norm_g.reshape(norm_g.shape[0], 1, d), mods, mods]
    specs += [g_spec, _mod_spec(layer, shift_chunk, bsz, ctx_tiles),
              _mod_spec(layer, scale_chunk, bsz, ctx_tiles)]
    for dt in out_dtypes:
        out_shape.append(jax.ShapeDtypeStruct(z.shape, dt))
        out_specs.append(tile)
    return pl.pallas_call(
        functools.partial(_norm_mod_kernel, has_delta=delta is not None, n_out=len(out_dtypes)),
        grid=(bsz, n // tm), in_specs=specs, out_specs=out_specs, out_shape=out_shape,
        compiler_params=_params(("arbitrary", "arbitrary"), 40),
        name="norm_mod",
    )(*args)


def _resid_kernel(z_ref, d_ref, gate_ref, o_ref):
    o_ref[0] = z_ref[0] + gate_ref[...] * d_ref[0]


def resid_add(z, delta, mods, layer, chunk, n_ctx):
    bsz, n, d = z.shape
    tm = SEQ_CHUNK
    tile = pl.BlockSpec((1, tm, d), lambda b, t: (b, t, 0))
    return pl.pallas_call(
        _resid_kernel, grid=(bsz, n // tm),
        in_specs=[tile, tile, _mod_spec(layer, chunk, bsz, n_ctx // tm)],
        out_specs=tile, out_shape=jax.ShapeDtypeStruct(z.shape, F32),
        compiler_params=_params(("arbitrary", "arbitrary"), 40),
        name="resid_add",
    )(z, delta, mods)


def _mm_kernel(x_ref, w_ref, b_ref, o_ref):
    acc = _dot(x_ref[0].astype(BF16), w_ref[...].astype(BF16)) + b_ref[...]
    o_ref[0] = acc.astype(o_ref.dtype)


def matmul_bias(x, w, bias, layer, tm, tn, out_dtype=F32):
    bsz, n, k = x.shape
    nc = w.shape[-1]
    return pl.pallas_call(
        _mm_kernel, grid=(nc // tn, bsz, n // tm),
        in_specs=[pl.BlockSpec((1, tm, k), lambda j, b, t: (b, t, 0)),
                  pl.BlockSpec((None, k, tn), lambda j, b, t: (layer, 0, j)),
                  pl.BlockSpec((None, 1, tn), lambda j, b, t: (layer, 0, j))],
        out_specs=pl.BlockSpec((1, tm, tn), lambda j, b, t: (b, t, j)),
        out_shape=jax.ShapeDtypeStruct((bsz, n, nc), out_dtype),
        compiler_params=_params(("arbitrary", "arbitrary", "arbitrary"), 48),
        name="matmul_bias",
    )(x, w, bias)


def _mm_resid_kernel(x_ref, w_ref, z_ref, gate_ref, o_ref):
    acc = _dot(x_ref[0].astype(BF16), w_ref[...].astype(BF16))
    o_ref[0] = z_ref[0] + gate_ref[...] * acc


def matmul_resid(x, w, z, mods, layer, gate_chunk, n_ctx):
    bsz, n, k = x.shape
    nc = w.shape[-1]
    tm, tn = SEQ_CHUNK, 512
    ctx_tiles = n_ctx // tm
    gate_spec = pl.BlockSpec(
        (None, None, 1, tn),
        lambda j, b, t: (layer, jnp.where(t < ctx_tiles, bsz, b), 0, gate_chunk * (nc // tn) + j))
    return pl.pallas_call(
        _mm_resid_kernel, grid=(nc // tn, bsz, n // tm),
        in_specs=[pl.BlockSpec((1, tm, k), lambda j, b, t: (b, t, 0)),
                  pl.BlockSpec((None, k, tn), lambda j, b, t: (layer, 0, j)),
                  pl.BlockSpec((1, tm, tn), lambda j, b, t: (b, t, j)),
                  gate_spec],
        out_specs=pl.BlockSpec((1, tm, tn), lambda j, b, t: (b, t, j)),
        out_shape=jax.ShapeDtypeStruct((bsz, n, nc), F32),
        compiler_params=_params(("arbitrary", "arbitrary", "arbitrary"), 48),
        name="matmul_resid",
    )(x, w, z, mods)


def _conv_kernel(x_ref, w_ref, b_ref, o_ref, ot_ref, *, n_ctx):
    x = x_ref[0]
    n = x.shape[0]
    row = lax.broadcasted_iota(jnp.int32, x.shape, 0)
    prev = jnp.where((row == 0) | (row == n_ctx), 0.0, pltpu.roll(x, 1, 0))
    nxt = jnp.where((row == n_ctx - 1) | (row == n - 1), 0.0, pltpu.roll(x, n - 1, 0))
    w = w_ref[...]
    y = b_ref[...] + prev * w[0:1] + x * w[1:2] + nxt * w[2:3]
    y = y * jax.nn.sigmoid(y)
    k_tiles = pl.num_programs(1) // 2
    y = y * jnp.where(pl.program_id(1) >= k_tiles, ML_DH ** -0.5, 1.0)
    o_ref[0] = y.astype(o_ref.dtype)
    ot_ref[0] = y.T.astype(ot_ref.dtype)


def conv_silu(proj, conv_w, conv_b, layer, n_ctx):
    bsz, n, _ = proj.shape
    tc = 256
    c2 = 2 * MIX_W
    return pl.pallas_call(
        functools.partial(_conv_kernel, n_ctx=n_ctx),
        grid=(bsz, c2 // tc),
        in_specs=[pl.BlockSpec((1, n, tc), lambda b, c: (b, 0, c)),
                  pl.BlockSpec((None, conv_w.shape[1], tc), lambda b, c: (layer, 0, c)),
                  pl.BlockSpec((None, 1, tc), lambda b, c: (layer, 0, c))],
        out_specs=[pl.BlockSpec((1, n, tc), lambda b, c: (b, 0, c)),
                   pl.BlockSpec((1, tc, n), lambda b, c: (b, c, 0))],
        out_shape=[jax.ShapeDtypeStruct((bsz, n, c2), BF16),
                   jax.ShapeDtypeStruct((bsz, c2, n), BF16)],
        compiler_params=_params(("arbitrary", "arbitrary"), 48),
        name="conv_silu",
    )(proj, conv_w, conv_b.reshape(conv_b.shape[0], 1, c2))


def _log_sigmoid(x):
    return jnp.minimum(x, 0.0) - jnp.log(1.0 + jnp.exp(-jnp.abs(x)))


def _mlstm_kernel(qf_ref, kf_ref, ktf_ref, vf_ref, gf_ref, gtf_ref,
                  qb_ref, kb_ref, ktb_ref, vb_ref, gb_ref, gtb_ref,
                  hf_ref, hb_ref, c_ref, m_ref):
    t = qf_ref.shape[1]
    dh = ML_DH

    @pl.when(pl.program_id(1) == 0)
    def _():
        c_ref[...] = jnp.zeros_like(c_ref)
        m_ref[...] = jnp.zeros_like(m_ref)

    r_i = lax.broadcasted_iota(jnp.int32, (t, t), 0)
    c_i = lax.broadcasted_iota(jnp.int32, (t, t), 1)
    low = c_i <= r_i
    upp = c_i >= r_i
    low_b = low.astype(BF16)
    upp_b = upp.astype(BF16)
    ones_col = (lax.broadcasted_iota(jnp.int32, (t, LANES), 1) == 0).astype(F32)

    dirs = ((qf_ref, kf_ref, ktf_ref, vf_ref, gf_ref, gtf_ref, hf_ref, low, low_b, upp_b),
            (qb_ref, kb_ref, ktb_ref, vb_ref, gb_ref, gtb_ref, hb_ref, upp, upp_b, low_b))
    for d, (q_ref, k_ref, kt_ref, v_ref, g_ref, gt_ref, h_ref, mask, col_tri, row_tri) in enumerate(dirs):
        g = g_ref[0]
        gt = gt_ref[0]
        lf = _log_sigmoid(g)
        lft = _log_sigmoid(gt)
        b_cols = sum(_dot(col_tri, p) for p in _split3(lf))
        b_rows = sum(_dot(p, row_tri) for p in _split3(lft))
        last = t - 1 if d == 0 else 0
        for hd in range(ML_HEADS):
            ci = 2 * d * ML_HEADS + hd
            cf = ci + ML_HEADS
            idx = d * ML_HEADS + hd
            sl = slice(hd * dh, (hd + 1) * dh)
            q = q_ref[0, :, sl]
            k = k_ref[0, :, sl]
            kt = kt_ref[0, sl, :]
            v = v_ref[0, :, sl]
            i_col = g[:, ci:ci + 1]
            i_row = gt[ci:ci + 1, :]
            b_col = b_cols[:, cf:cf + 1]
            b_row = b_rows[cf:cf + 1, :]
            b_tot = b_col[last:last + 1, :]
            m_prev = m_ref[idx][:, :1]
            c_prev = c_ref[idx]

            log_inter = b_col + m_prev
            log_intra = jnp.where(mask, b_col - b_row + i_row, -jnp.inf)
            m_t = jnp.maximum(log_inter, jnp.max(log_intra, axis=-1, keepdims=True))
            w_inter = jnp.exp(log_inter - m_t)
            s = _dot_nt(q, k) * jnp.exp(log_intra - m_t)
            v_ext = jnp.concatenate([v, ones_col], axis=1)
            nd = w_inter * _dot(q, c_prev.astype(BF16)) + _dot(s.astype(BF16), v_ext.astype(BF16))
            den = nd[:, dh:dh + 1]
            h_ref[0, :, sl] = nd[:, :dh] * (1.0 / jnp.maximum(jnp.abs(den), jnp.exp(-m_t)))

            log_w = b_tot - b_col + i_col
            m_new = jnp.maximum(b_tot + m_prev, jnp.max(log_w, axis=0, keepdims=True))
            decay = jnp.exp(b_tot + m_prev - m_new)
            w_s = jnp.exp(log_w - m_new)
            c_ref[idx] = decay * c_prev + _dot(kt, (v_ext * w_s).astype(BF16))
            m_ref[idx] = jnp.broadcast_to(m_new, (1, LANES))


def mlstm(qk, qkt, proj, gates, gates_t, n_ctx):
    bsz, n, _ = qk.shape
    t = SEQ_CHUNK
    nc = n // t
    c0 = n_ctx // t
    w = MIX_W
    fwd = lambda j: j
    bwd = lambda j: _rev_chunk(j, c0, nc)

    def specs(ch):
        return [pl.BlockSpec((1, t, w), lambda b, j: (b, ch(j), 0)),
                pl.BlockSpec((1, t, w), lambda b, j: (b, ch(j), 1)),
                pl.BlockSpec((1, w, t), lambda b, j: (b, 1, ch(j))),
                pl.BlockSpec((1, t, w), lambda b, j: (b, ch(j), COL_VA // w)),
                pl.BlockSpec((1, t, LANES), lambda b, j: (b, ch(j), 0)),
                pl.BlockSpec((1, N_GATES, t), lambda b, j: (b, 0, ch(j)))]

    args = [qk, qk, qkt, proj, gates, gates_t]
    out = jax.ShapeDtypeStruct((bsz, n, w), F32)
    return pl.pallas_call(
        _mlstm_kernel, grid=(bsz, nc),
        in_specs=specs(fwd) + specs(bwd),
        out_specs=[pl.BlockSpec((1, t, w), lambda b, j: (b, fwd(j), 0)),
                   pl.BlockSpec((1, t, w), lambda b, j: (b, bwd(j), 0))],
        out_shape=[out, out],
        scratch_shapes=[pltpu.VMEM((2 * ML_HEADS, ML_DH, ML_DH + LANES), F32),
                        pltpu.VMEM((2 * ML_HEADS, 1, LANES), F32)],
        compiler_params=_params(("arbitrary", "arbitrary"), 48),
        name="mlstm",
    )(*(args + args))


def _ya_kernel(hf_ref, hb_ref, oa_ref, g_ref, o_ref):
    h = hf_ref[0] + hb_ref[0]
    parts = []
    for hd in range(ML_HEADS):
        x = h[:, hd * ML_DH:(hd + 1) * ML_DH]
        parts.append(x * lax.rsqrt(jnp.mean(x * x, axis=-1, keepdims=True) + EPS))
    y = jnp.concatenate(parts, axis=1) * g_ref[...]
    o_ref[0] = (jax.nn.sigmoid(oa_ref[0]) * y).astype(o_ref.dtype)


def mlstm_out(hf, hb, proj, norm_g, layer):
    bsz, n, w = hf.shape
    tm = SEQ_CHUNK
    tile = pl.BlockSpec((1, tm, w), lambda b, t: (b, t, 0))
    return pl.pallas_call(
        _ya_kernel, grid=(bsz, n // tm),
        in_specs=[tile, tile, pl.BlockSpec((1, tm, w), lambda b, t: (b, t, COL_OA // w)),
                  pl.BlockSpec((None, 1, w), lambda b, t: (layer, 0, 0))],
        out_specs=tile, out_shape=jax.ShapeDtypeStruct((bsz, n, w), BF16),
        compiler_params=_params(("arbitrary", "arbitrary"), 40),
        name="mlstm_out",
    )(hf, hb, proj, norm_g.reshape(norm_g.shape[0], 1, w))


def _s5_kernel(uf_ref, ub_ref, bm_ref, cm_ref, tab_ref, yf_ref, yb_ref, re_ref, im_ref, carry_ref):
    t = uf_ref.shape[1]
    nblk = t // SUBLANES
    bw = S5_BUNDLE * S5_GROUP
    sw = S5_BUNDLE * S5_STATE

    @pl.when(pl.program_id(1) == 0)
    def _():
        carry_ref[...] = jnp.zeros_like(carry_ref)

    for d, u_ref in enumerate((uf_ref, ub_ref)):
        u = u_ref[0].astype(BF16)
        for kb in range(S5_NB):
            bu = _dot(u[:, kb * bw:(kb + 1) * bw], bm_ref[d, kb])
            re_ref[d, :, kb * sw:(kb + 1) * sw] = bu[:, :sw]
            im_ref[d, :, kb * sw:(kb + 1) * sw] = bu[:, sw:]

    def body(i, _):
        for d in range(2):
            rb = i if d == 0 else nblk - 1 - i
            r0 = pl.multiple_of(rb * SUBLANES, SUBLANES)
            xr = re_ref[d, pl.ds(r0, SUBLANES), :]
            xi = im_ref[d, pl.ds(r0, SUBLANES), :]
            for lvl, sh in enumerate((1, 2, 4)):
                ar = tab_ref[d, 2 * lvl]
                ai = tab_ref[d, 2 * lvl + 1]
                amt = sh if d == 0 else SUBLANES - sh
                sr = pltpu.roll(xr, amt, 0)
                si = pltpu.roll(xi, amt, 0)
                xr, xi = xr + ar * sr - ai * si, xi + ar * si + ai * sr
            pr = tab_ref[d, 6]
            pi = tab_ref[d, 7]
            cr = carry_ref[d, 0]
            ci = carry_ref[d, 1]
            xr, xi = xr + pr * cr - pi * ci, xi + pr * ci + pi * cr
            re_ref[d, pl.ds(r0, SUBLANES), :] = xr
            im_ref[d, pl.ds(r0, SUBLANES), :] = xi
            last = SUBLANES - 1 if d == 0 else 0
            carry_ref[d, 0] = jnp.broadcast_to(xr[last:last + 1, :], xr.shape)
            carry_ref[d, 1] = jnp.broadcast_to(xi[last:last + 1, :], xi.shape)
        return 0

    lax.fori_loop(0, nblk, body, 0)

    for d, y_ref in enumerate((yf_ref, yb_ref)):
        for kb in range(S5_NB):
            sr = re_ref[d, :, kb * sw:(kb + 1) * sw].astype(BF16)
            si = im_ref[d, :, kb * sw:(kb + 1) * sw].astype(BF16)
            y_ref[0, :, kb * bw:(kb + 1) * bw] = _dot(sr, cm_ref[kb, 0]) + _dot(si, cm_ref[kb, 1])


def s5_scan(proj, bmat, cmat, tab, n_ctx):
    bsz, n, _ = proj.shape
    t = 128
    nc = n // t
    c0 = n_ctx // t
    w = MIX_W
    bwd = lambda j: _rev_chunk(j, c0, nc)
    out = jax.ShapeDtypeStruct((bsz, n, w), F32)
    const = lambda a: pl.BlockSpec(a.shape, lambda b, j: (0,) * a.ndim)
    return pl.pallas_call(
        _s5_kernel, grid=(bsz, nc),
        in_specs=[pl.BlockSpec((1, t, w), lambda b, j: (b, j, COL_US // w)),
                  pl.BlockSpec((1, t, w), lambda b, j: (b, bwd(j), COL_US // w)),
                  const(bmat), const(cmat), const(tab)],
        out_specs=[pl.BlockSpec((1, t, w), lambda b, j: (b, j, 0)),
                   pl.BlockSpec((1, t, w), lambda b, j: (b, bwd(j), 0))],
        out_shape=[out, out],
        scratch_shapes=[pltpu.VMEM((2, t, S5_NS), F32), pltpu.VMEM((2, t, S5_NS), F32),
                        pltpu.VMEM((2, 2, SUBLANES, S5_NS), F32)],
        compiler_params=_params(("arbitrary", "arbitrary"), 56),
        name="s5_scan",
    )(proj, proj, bmat, cmat, tab)


def s5_operands(lam_re, lam_im, log_dt, b_re, b_im, c_re, c_im):
    lam_re, lam_im = lam_re.astype(F32), lam_im.astype(F32)
    dt = jnp.exp(log_dt.astype(F32))[..., None]
    mag = jnp.exp(lam_re * dt)
    a_re, a_im = mag * jnp.cos(lam_im * dt), mag * jnp.sin(lam_im * dt)
    nr, ni = a_re - 1.0, a_im
    den = lam_re * lam_re + lam_im * lam_im
    fr = (nr * lam_re + ni * lam_im) / den
    fi = (ni * lam_re - nr * lam_im) / den
    bb_re = fr[..., None] * b_re - fi[..., None] * b_im
    bb_im = fr[..., None] * b_im + fi[..., None] * b_re
    eye = jnp.eye(S5_BUNDLE, dtype=F32)

    def blockdiag_b(bb):
        x = bb.reshape(2, S5_NB, S5_BUNDLE, S5_STATE, S5_GROUP)
        return jnp.einsum('dkgpi,gh->dkgihp', x, eye).reshape(
            2, S5_NB, S5_BUNDLE * S5_GROUP, S5_BUNDLE * S5_STATE)

    bmat = jnp.concatenate([blockdiag_b(bb_re), blockdiag_b(bb_im)], -1).astype(BF16)

    def blockdiag_c(c):
        x = c.astype(F32).reshape(S5_NB, S5_BUNDLE, S5_GROUP, S5_STATE)
        return jnp.einsum('kgip,gh->kgphi', x, eye).reshape(
            S5_NB, S5_BUNDLE * S5_STATE, S5_BUNDLE * S5_GROUP)

    cmat = jnp.stack([blockdiag_c(c_re), -blockdiag_c(c_im)], 1).astype(BF16)

    ar, ai = a_re.reshape(2, S5_NS), a_im.reshape(2, S5_NS)
    pows = [(jnp.ones_like(ar), jnp.zeros_like(ai))]
    for _ in range(SUBLANES):
        pr, pi = pows[-1]
        pows.append((pr * ar - pi * ai, pr * ai + pi * ar))
    rows = jnp.arange(SUBLANES)[:, None]
    tabs = []
    for d in range(2):
        ents = []
        for sh in (1, 2, 4):
            valid = (rows >= sh) if d == 0 else (rows <= SUBLANES - 1 - sh)
            for comp in range(2):
                ents.append(jnp.where(valid, pows[sh][comp][d][None, :], 0.0))
        expo = (rows + 1) if d == 0 else (SUBLANES - rows)
        for comp in range(2):
            stacked = jnp.stack([p[comp][d] for p in pows], 0)
            ents.append(jnp.take(stacked, expo[:, 0], axis=0))
        tabs.append(jnp.stack(ents, 0))
    tab = jnp.stack(tabs, 0)
    return bmat, cmat, tab


def _glu_kernel(yf_ref, yb_ref, u_ref, d_ref, w_ref, b_ref, o_ref):
    y = d_ref[...] * u_ref[0] + yf_ref[0] + yb_ref[0]
    z = jax.nn.gelu(y)
    o_ref[0] = (z * jax.nn.sigmoid(_dot(z.astype(BF16), w_ref[...].astype(BF16)) + b_ref[...])).astype(o_ref.dtype)


def s5_glu(yf, yb, proj, d_skip, glu_w, glu_b, layer):
    bsz, n, w = yf.shape
    tm = SEQ_CHUNK
    tile = pl.BlockSpec((1, tm, w), lambda b, t: (b, t, 0))
    vec = pl.BlockSpec((None, 1, w), lambda b, t: (layer, 0, 0))
    return pl.pallas_call(
        _glu_kernel, grid=(bsz, n // tm),
        in_specs=[tile, tile, pl.BlockSpec((1, tm, w), lambda b, t: (b, t, COL_US // w)),
                  vec, pl.BlockSpec((None, w, w), lambda b, t: (layer, 0, 0)), vec],
        out_specs=tile, out_shape=jax.ShapeDtypeStruct((bsz, n, w), BF16),
        compiler_params=_params(("arbitrary", "arbitrary"), 40),
        name="s5_glu",
    )(yf, yb, proj, d_skip.reshape(-1, 1, w), glu_w, glu_b.reshape(-1, 1, w))


def _da_prep_kernel(x_ref, g_ref, cos_ref, sin_ref, o_ref):
    x = x_ref[0]
    lane = lax.broadcasted_iota(jnp.int32, x.shape, 1)
    first = lane < DA_DQK
    x2 = x * x
    ms0 = jnp.sum(jnp.where(first, x2, 0.0), axis=-1, keepdims=True)
    ms1 = jnp.sum(jnp.where(first, 0.0, x2), axis=-1, keepdims=True)
    ms = jnp.where(first, ms0, ms1) * (1.0 / DA_DQK)
    y = x * lax.rsqrt(ms + EPS) * g_ref[...]
    quarter = DA_DQK // 4
    partner = jnp.where((lane & quarter) == 0, pltpu.roll(y, LANES - quarter, 1), pltpu.roll(y, quarter, 1))
    r = y * cos_ref[...] + partner * sin_ref[...]
    q_heads = pl.num_programs(2) // 2
    r = r * jnp.where(pl.program_id(2) < q_heads, DA_DQK ** -0.5, 1.0)
    o_ref[0] = r.astype(o_ref.dtype)


def da_prep(proj, qn_g, kn_g, cos_t, sin_t, layer):
    bsz, n, _ = proj.shape
    tm = SEQ_CHUNK
    nh = 2 * DA_HEADS
    gains = jnp.concatenate([jnp.tile(qn_g, (1, 2 * DA_HEADS)), jnp.tile(kn_g, (1, 2 * DA_HEADS))], -1)
    gains = gains.reshape(gains.shape[0], 1, nh * LANES)
    return pl.pallas_call(
        _da_prep_kernel, grid=(bsz, n // tm, nh),
        in_specs=[pl.BlockSpec((1, tm, LANES), lambda b, t, h: (b, t, COL_QD // LANES + h)),
                  pl.BlockSpec((None, 1, LANES), lambda b, t, h: (layer, 0, h)),
                  pl.BlockSpec((tm, LANES), lambda b, t, h: (t, 0)),
                  pl.BlockSpec((tm, LANES), lambda b, t, h: (t, 0))],
        out_specs=pl.BlockSpec((1, tm, LANES), lambda b, t, h: (b, t, h)),
        out_shape=jax.ShapeDtypeStruct((bsz, n, nh * LANES), BF16),
        compiler_params=_params(("arbitrary", "arbitrary", "arbitrary"), 40),
        name="da_prep",
    )(proj, gains, cos_t, sin_t)


def rope_tables(n, n_ctx):
    lat = jnp.arange(n) - n_ctx
    rows = (lat // GRID_W).astype(F32)
    cols = (lat % GRID_W).astype(F32)
    lane = jnp.arange(LANES)
    in_comp = lane % DA_DQK
    quarter = DA_DQK // 4
    freqs = ROPE_BASE ** (-(in_comp % quarter).astype(F32) / quarter)
    pos = jnp.where((in_comp < DA_DQK // 2)[None, :], rows[:, None], cols[:, None])
    ang = pos * freqs[None, :]
    sign = jnp.where((in_comp % (2 * quarter)) < quarter, -1.0, 1.0)
    is_lat = (lat >= 0)[:, None]
    cos_t = jnp.where(is_lat, jnp.cos(ang), 1.0)
    sin_t = jnp.where(is_lat, jnp.sin(ang) * sign[None, :], 0.0)
    return cos_t.astype(F32), sin_t.astype(F32)


def _softmax_rows(s):
    e = jnp.exp(s - jnp.max(s, axis=-1, keepdims=True))
    return e * (1.0 / jnp.sum(e, axis=-1, keepdims=True))


def _diff_attn_kernel(scal_ref, q_ref, k_ref, v_ref, g_ref, o_ref, *, n_ctx, ctx_tiles):
    lam = scal_ref[0]
    out_scale = scal_ref[1]
    q = q_ref[0]
    lane = lax.broadcasted_iota(jnp.int32, q.shape, 1)
    zero = jnp.zeros_like(q)
    q0 = jnp.where(lane < DA_DQK, q, zero)
    q1 = jnp.where(lane < DA_DQK, zero, q)

    def attend(k, v):
        p0 = _softmax_rows(_dot_nt(q0, k))
        p1 = _softmax_rows(_dot_nt(q1, k))
        o = _dot((p0 - lam * p1).astype(BF16), v.astype(BF16))
        o = o * lax.rsqrt(jnp.mean(o * o, axis=-1, keepdims=True) + EPS) * g_ref[...]
        o_ref[0] = (o * out_scale).astype(o_ref.dtype)

    @pl.when(pl.program_id(2) < ctx_tiles)
    def _():
        attend(k_ref[0, :n_ctx, :], v_ref[0, :n_ctx, :])

    @pl.when(pl.program_id(2) >= ctx_tiles)
    def _():
        attend(k_ref[0], v_ref[0])


def diff_attn(qk_hat, proj, scal, sub_g, layer, n_ctx):
    bsz, n, _ = qk_hat.shape
    tq = SEQ_CHUNK
    return pl.pallas_call(
        functools.partial(_diff_attn_kernel, n_ctx=n_ctx, ctx_tiles=n_ctx // tq),
        grid=(bsz, DA_HEADS, n // tq),
        in_specs=[pl.BlockSpec(memory_space=pltpu.SMEM),
                  pl.BlockSpec((1, tq, LANES), lambda b, h, t: (b, t, h)),
                  pl.BlockSpec((1, n, LANES), lambda b, h, t: (b, 0, DA_HEADS + h)),
                  pl.BlockSpec((1, n, LANES), lambda b, h, t: (b, 0, COL_VD // LANES + h)),
                  pl.BlockSpec((None, 1, LANES), lambda b, h, t: (layer, 0, 0))],
        out_specs=pl.BlockSpec((1, tq, LANES), lambda b, h, t: (b, t, h)),
        out_shape=jax.ShapeDtypeStruct((bsz, n, DA_HEADS * DA_DV), BF16),
        compiler_params=_params(("arbitrary", "arbitrary", "arbitrary"), 48),
        name="diff_attn",
    )(scal, qk_hat, qk_hat, proj, sub_g.reshape(sub_g.shape[0], 1, DA_DV))


def _merge_kernel(ya_ref, yb_ref, yc_ref, p_ref, ga_ref, gb_ref, gc_ref, o_ref):
    acc = _dot(ya_ref[0], p_ref[0].astype(BF16)) * jax.nn.sigmoid(ga_ref[0])
    acc += _dot(yb_ref[0], p_ref[1].astype(BF16)) * jax.nn.sigmoid(gb_ref[0])
    acc += _dot(yc_ref[0], p_ref[2].astype(BF16)) * jax.nn.sigmoid(gc_ref[0])
    o_ref[0] = acc.astype(o_ref.dtype)


def branch_merge(ya, yb, yc, proj, branch_proj, layer):
    bsz, n, w = ya.shape
    d = branch_proj.shape[-1]
    tm, tn = 768, 512
    y_tile = pl.BlockSpec((1, tm, w), lambda j, b, t: (b, t, 0))
    gate = lambda r: pl.BlockSpec((1, tm, tn), lambda j, b, t: (b, t, (COL_GB + r * d) // tn + j))
    return pl.pallas_call(
        _merge_kernel, grid=(d // tn, bsz, n // tm),
        in_specs=[y_tile, y_tile, y_tile,
                  pl.BlockSpec((None, N_BRANCH, w, tn), lambda j, b, t: (layer, 0, 0, j)),
                  gate(0), gate(1), gate(2)],
        out_specs=pl.BlockSpec((1, tm, tn), lambda j, b, t: (b, t, j)),
        out_shape=jax.ShapeDtypeStruct((bsz, n, d), BF16),
        compiler_params=_params(("arbitrary", "arbitrary", "arbitrary"), 48),
        name="branch_merge",
    )(ya, yb, yc, branch_proj, proj, proj, proj)


def _router_kernel(h_ref, w_ref, b_ref, o_ref):
    h_parts = _split3(h_ref[0])
    w_parts = _split3(w_ref[...])
    logits = (_dot_nt(w_parts[0], h_parts[0]) + _dot_nt(w_parts[0], h_parts[1])
              + _dot_nt(w_parts[1], h_parts[0]))
    score = jax.nn.sigmoid(logits)
    sel = score + b_ref[...]
    rows = [sel[e:e + 1, :] for e in range(N_EXPERTS)]
    gscores = []
    for g in range(N_EXPERT_GROUPS):
        a, b, c, d = rows[g * EXPERTS_PER_GROUP:(g + 1) * EXPERTS_PER_GROUP]
        hi1, lo1 = jnp.maximum(a, b), jnp.minimum(a, b)
        hi2, lo2 = jnp.maximum(c, d), jnp.minimum(c, d)
        gscores.append(jnp.maximum(hi1, hi2) + jnp.maximum(jnp.minimum(hi1, hi2), jnp.maximum(lo1, lo2)))
    gmax = functools.reduce(jnp.maximum, gscores)
    taken = jnp.zeros_like(gmax, dtype=jnp.bool_)
    picked = []
    for g in range(N_EXPERT_GROUPS):
        is_g = (gscores[g] == gmax) & jnp.logical_not(taken)
        picked.append(is_g)
        taken = taken | is_g
    chosen = []
    for e in range(N_EXPERTS):
        g = e // EXPERTS_PER_GROUP
        rank = jnp.zeros_like(gmax)
        for o in range(g * EXPERTS_PER_GROUP, (g + 1) * EXPERTS_PER_GROUP):
            if o == e:
                continue
            ahead = (rows[o] > rows[e]) | ((rows[o] == rows[e]) & (o < e))
            rank = rank + ahead.astype(F32)
        chosen.append(jnp.where(picked[g] & (rank < 2.0), score[e:e + 1, :], 0.0))
    total = functools.reduce(jnp.add, chosen)
    gates_t = jnp.concatenate(chosen, axis=0) * (1.0 / total)
    pad = jnp.zeros((LANES - N_EXPERTS, gates_t.shape[1]), F32)
    o_ref[0] = jnp.concatenate([gates_t, pad], axis=0).T


def moe_router(h2, router_w, router_b):
    bsz, n, d = h2.shape
    tm = SEQ_CHUNK
    return pl.pallas_call(
        _router_kernel, grid=(bsz, n // tm),
        in_specs=[pl.BlockSpec((1, tm, d), lambda b, t: (b, t, 0)),
                  pl.BlockSpec((N_EXPERTS, d), lambda b, t: (0, 0)),
                  pl.BlockSpec((N_EXPERTS, 1), lambda b, t: (0, 0))],
        out_specs=pl.BlockSpec((1, tm, LANES), lambda b, t: (b, t, 0)),
        out_shape=jax.ShapeDtypeStruct((bsz, n, LANES), F32),
        compiler_params=_params(("arbitrary", "arbitrary"), 40),
        name="moe_router",
    )(h2, router_w.T, router_b.reshape(N_EXPERTS, 1))


def _moe_kernel(x_ref, g_ref, wg_ref, wu_ref, wd_ref, o_ref):
    e = pl.program_id(2)
    x = x_ref[0]
    gates = g_ref[0]
    lane = lax.broadcasted_iota(jnp.int32, gates.shape, 1)
    gate = jnp.sum(jnp.where(lane == e, gates, 0.0), axis=-1, keepdims=True)
    a = _dot(x, wg_ref[...].astype(BF16))
    act = (a * jax.nn.sigmoid(a)) * _dot(x, wu_ref[...].astype(BF16)) * gate
    y = _dot(act.astype(BF16), wd_ref[...].astype(BF16))

    @pl.when(e == 0)
    def _():
        o_ref[0] = y

    @pl.when(e > 0)
    def _():
        o_ref[0] += y


def moe_dense(h2, gates, w_gate, w_up, w_down, layer):
    bsz, n, d = h2.shape
    f = w_gate.shape[-1]
    tm = n // 3
    return pl.pallas_call(
        _moe_kernel, grid=(bsz, n // tm, N_EXPERTS),
        in_specs=[pl.BlockSpec((1, tm, d), lambda b, t, e: (b, t, 0)),
                  pl.BlockSpec((1, tm, LANES), lambda b, t, e: (b, t, 0)),
                  pl.BlockSpec((None, None, d, f), lambda b, t, e: (layer, e, 0, 0)),
                  pl.BlockSpec((None, None, d, f), lambda b, t, e: (layer, e, 0, 0)),
                  pl.BlockSpec((None, None, f, d), lambda b, t, e: (layer, e, 0, 0))],
        out_specs=pl.BlockSpec((1, tm, d), lambda b, t, e: (b, t, 0)),
        out_shape=jax.ShapeDtypeStruct((bsz, n, d), F32),
        compiler_params=_params(("arbitrary", "arbitrary", "arbitrary"), 56),
        name="moe_dense",
    )(h2, gates, w_gate, w_up, w_down)


def kernel(x, c, ctx, c_ctx, norm1_g, norm2_g, ada_w, ada_b, w_in, b_in, ml_conv_w, ml_conv_b, ml_norm_g, s5_lam_re, s5_lam_im, s5_log_dt, s5_b_re, s5_b_im, s5_c_re, s5_c_im, s5_d, s5_glu_w, s5_glu_b, da_q_norm_g, da_k_norm_g, da_lam_q1, da_lam_k1, da_lam_q2, da_lam_k2, da_sub_norm_g, branch_proj, w_out, router_w, router_b, exp_w_gate, exp_w_up, exp_w_down):
    bsz, n_lat, d = x.shape
    n_ctx = ctx.shape[1]
    n = n_ctx + n_lat
    depth = w_in.shape[0]
    assert n_ctx % SEQ_CHUNK == 0 and n_lat % SEQ_CHUNK == 0 and d == D_MODEL

    rows = -(-(bsz + 1) // SUBLANES) * SUBLANES
    cvec = jnp.zeros((rows, d), F32).at[:bsz].set(c).at[bsz].set(c_ctx)
    mods = ada_mod(cvec, ada_w, ada_b).reshape(depth, rows, 1, 6 * d)

    w_main = jnp.concatenate([w_in[..., :GATE_OFF], w_in[..., GATE_OFF + N_GATES:]], -1).astype(BF16)
    b_main = jnp.concatenate([b_in[..., :GATE_OFF], b_in[..., GATE_OFF + N_GATES:]], -1)[:, None, :]
    w_gates = jnp.pad(w_in[..., GATE_OFF:GATE_OFF + N_GATES], ((0, 0), (0, 0), (0, LANES - N_GATES))).astype(BF16)
    b_gates = jnp.pad(b_in[..., GATE_OFF:GATE_OFF + N_GATES], ((0, 0), (0, LANES - N_GATES)))[:, None, :]

    e_gate, e_up, e_down = exp_w_gate.astype(BF16), exp_w_up.astype(BF16), exp_w_down.astype(BF16)
    cos_t, sin_t = rope_tables(n, n_ctx)
    z = jnp.concatenate([ctx, x], 1)
    moe_out = None
    for i in range(depth):
        lam_init = 0.8 - 0.6 * math.exp(-0.3 * i)
        if moe_out is None:
            (h,) = norm_mod(z, norm1_g, mods, i, 0, 1, n_ctx, (BF16,))
        else:
            z, h = norm_mod(z, norm1_g, mods, i, 0, 1, n_ctx, (BF16,), delta=moe_out, gate_layer=i - 1, gate_chunk=5)
        proj = matmul_bias(h, w_main, b_main, i, n // 2, 1024)
        gates = matmul_bias(h, w_gates, b_gates, i, n // 2, LANES)
        gates_t = jnp.swapaxes(gates[..., :N_GATES], 1, 2)

        qk, qkt = conv_silu(proj, ml_conv_w, ml_conv_b, i, n_ctx)
        hf, hb = mlstm(qk, qkt, proj, gates, gates_t, n_ctx)
        ya = mlstm_out(hf, hb, proj, ml_norm_g, i)

        bmat, cmat, tab = s5_operands(s5_lam_re[i], s5_lam_im[i], s5_log_dt[i], s5_b_re[i], s5_b_im[i],
                                      s5_c_re[i], s5_c_im[i])
        yf, ybk = s5_scan(proj, bmat, cmat, tab, n_ctx)
        yb = s5_glu(yf, ybk, proj, s5_d, s5_glu_w, s5_glu_b, i)

        qk_hat = da_prep(proj, da_q_norm_g, da_k_norm_g, cos_t, sin_t, i)
        lam = (jnp.exp(jnp.sum(da_lam_q1[i].astype(F32) * da_lam_k1[i].astype(F32)))
               - jnp.exp(jnp.sum(da_lam_q2[i].astype(F32) * da_lam_k2[i].astype(F32))) + lam_init)
        scal = jnp.stack([lam, jnp.asarray(1.0 - lam_init, F32)]).astype(F32)
        yc = diff_attn(qk_hat, proj, scal, da_sub_norm_g, i, n_ctx)

        merged = branch_merge(ya, yb, yc, proj, branch_proj, i)
        z = matmul_resid(merged, w_out, z, mods, i, 2, n_ctx)

        h2, h2b = norm_mod(z, norm2_g, mods, i, 3, 4, n_ctx, (F32, BF16))
        gates_moe = moe_router(h2, router_w, router_b)
        moe_out = moe_dense(h2b, gates_moe, e_gate, e_up, e_down, i)
    z = resid_add(z, moe_out, mods, depth - 1, 5, n_ctx)
    return z[:, n_ctx:]
```

```python
import functools
import math

import jax
import jax.numpy as jnp
from jax import lax
from jax.experimental import pallas as pl
from jax.experimental.pallas import tpu as pltpu

F32 = jnp.float32
BF16 = jnp.bfloat16

D_MODEL = 2048
GRID_W = 64
MIX_W = 1024
N_BRANCH = 3
ML_HEADS = 4
ML_DH = MIX_W // ML_HEADS
S5_GROUP = 16
S5_GROUPS = MIX_W // S5_GROUP
S5_STATE = 64
S5_BUNDLE = 16
S5_NB = S5_GROUPS // S5_BUNDLE
S5_NS = S5_GROUPS * S5_STATE
DA_HEADS = 8
DA_DQK = 64
DA_DV = 2 * DA_DQK
ROPE_BASE = 10000.0
N_EXPERTS = 16
N_EXPERT_GROUPS = 4
EXPERTS_PER_GROUP = N_EXPERTS // N_EXPERT_GROUPS
TOP_K = 2
D_FF_EXPERT = 512
EPS = 1e-6

COL_QA, COL_KA, COL_VA, COL_OA, COL_US, COL_QD, COL_KD, COL_VD, COL_GB = (
    0, 1024, 2048, 3072, 4096, 5120, 6144, 7168, 8192)
MAIN_COLS = 8192 + N_BRANCH * D_MODEL
GATE_OFF = 4 * MIX_W
N_GATES = 4 * ML_HEADS

SEQ_CHUNK = 256
LANES = 128
SUBLANES = 8
MIB = 1024 * 1024

S5_TC = 64
S5_HALVES = 2
S5_SLABS = S5_NS // S5_HALVES // LANES
S5_GRP = 4
MOE_TILE = 256


def _params(sem, vmem_mib):
    return pltpu.CompilerParams(dimension_semantics=sem, vmem_limit_bytes=vmem_mib * MIB)


def _dot(a, b):
    return jnp.dot(a, b, preferred_element_type=F32)


def _dot_nt(a, b):
    return lax.dot_general(a, b, (((1,), (1,)), ((), ())), preferred_element_type=F32)


def _split3(x):
    hi = x.astype(BF16)
    r = x - hi.astype(F32)
    mid = r.astype(BF16)
    lo = (r - mid.astype(F32)).astype(BF16)
    return hi, mid, lo


def _rev_chunk(j, n_ctx_chunks, n_chunks):
    return jnp.where(j < n_ctx_chunks, n_ctx_chunks - 1 - j, n_chunks - 1 - (j - n_ctx_chunks))


def _ada_kernel(c_ref, w_ref, b_ref, o_ref):
    c = c_ref[...]
    s = (c * jax.nn.sigmoid(c)).astype(BF16)
    o_ref[0] = _dot(s, w_ref[0].astype(BF16)) + b_ref[0]


def ada_mod(cvec, ada_w, ada_b):
    depth, d, n6 = ada_w.shape
    rows = cvec.shape[0]
    tn = 1024
    return pl.pallas_call(
        _ada_kernel,
        grid=(depth, n6 // tn),
        in_specs=[pl.BlockSpec((rows, d), lambda l, j: (0, 0)),
                  pl.BlockSpec((1, d, tn), lambda l, j: (l, 0, j)),
                  pl.BlockSpec((1, 1, tn), lambda l, j: (l, 0, j))],
        out_specs=pl.BlockSpec((1, rows, tn), lambda l, j: (l, 0, j)),
        out_shape=jax.ShapeDtypeStruct((depth, rows, n6), F32),
        compiler_params=_params(("arbitrary", "arbitrary"), 40),
        name="ada_mod",
    )(cvec, ada_w, ada_b.reshape(depth, 1, n6))


def _mod_spec(layer, chunk, batch, ctx_tiles):
    return pl.BlockSpec((None, None, 1, D_MODEL),
                        lambda b, t: (layer, jnp.where(t < ctx_tiles, batch, b), 0, chunk))


def _norm_mod_kernel(z_ref, g_ref, shift_ref, scale_ref, *outs):
    z = z_ref[0]
    y = z * lax.rsqrt(jnp.mean(z * z, axis=-1, keepdims=True) + EPS) * g_ref[...]
    h = y * (1.0 + scale_ref[...]) + shift_ref[...]
    for o in outs:
        o[0] = h.astype(o.dtype)


def norm_mod(z, norm_g, mods, layer, shift_chunk, scale_chunk, n_ctx, out_dtypes):
    bsz, n, d = z.shape
    tm = SEQ_CHUNK
    ctx_tiles = n_ctx // tm
    tile = pl.BlockSpec((1, tm, d), lambda b, t: (b, t, 0))
    return pl.pallas_call(
        _norm_mod_kernel, grid=(bsz, n // tm),
        in_specs=[tile, pl.BlockSpec((None, 1, d), lambda b, t: (layer, 0, 0)),
                  _mod_spec(layer, shift_chunk, bsz, ctx_tiles), _mod_spec(layer, scale_chunk, bsz, ctx_tiles)],
        out_specs=[tile] * len(out_dtypes),
        out_shape=[jax.ShapeDtypeStruct(z.shape, dt) for dt in out_dtypes],
        compiler_params=_params(("arbitrary", "arbitrary"), 40),
        name="norm_mod",
    )(z, norm_g.reshape(norm_g.shape[0], 1, d), mods, mods)


def _mm_kernel(x_ref, w_ref, b_ref, o_ref):
    acc = _dot(x_ref[0].astype(BF16), w_ref[...].astype(BF16)) + b_ref[...]
    o_ref[0] = acc.astype(o_ref.dtype)


def matmul_bias(x, w, bias, layer, tm, tn, out_dtype=F32):
    bsz, n, k = x.shape
    nc = w.shape[-1]
    return pl.pallas_call(
        _mm_kernel, grid=(nc // tn, bsz, n // tm),
        in_specs=[pl.BlockSpec((1, tm, k), lambda j, b, t: (b, t, 0)),
                  pl.BlockSpec((None, k, tn), lambda j, b, t: (layer, 0, j)),
                  pl.BlockSpec((None, 1, tn), lambda j, b, t: (layer, 0, j))],
        out_specs=pl.BlockSpec((1, tm, tn), lambda j, b, t: (b, t, j)),
        out_shape=jax.ShapeDtypeStruct((bsz, n, nc), out_dtype),
        compiler_params=_params(("arbitrary", "arbitrary", "arbitrary"), 48),
        name="matmul_bias",
    )(x, w, bias)


def _mm_resid_kernel(x_ref, w_ref, z_ref, gate_ref, o_ref):
    acc = _dot(x_ref[0].astype(BF16), w_ref[...].astype(BF16))
    o_ref[0] = z_ref[0] + gate_ref[...] * acc


def matmul_resid(x, w, z, mods, layer, gate_chunk, n_ctx):
    bsz, n, k = x.shape
    nc = w.shape[-1]
    tm, tn = SEQ_CHUNK, 512
    ctx_tiles = n_ctx // tm
    gate_spec = pl.BlockSpec(
        (None, None, 1, tn),
        lambda j, b, t: (layer, jnp.where(t < ctx_tiles, bsz, b), 0, gate_chunk * (nc // tn) + j))
    return pl.pallas_call(
        _mm_resid_kernel, grid=(nc // tn, bsz, n // tm),
        in_specs=[pl.BlockSpec((1, tm, k), lambda j, b, t: (b, t, 0)),
                  pl.BlockSpec((None, k, tn), lambda j, b, t: (layer, 0, j)),
                  pl.BlockSpec((1, tm, tn), lambda j, b, t: (b, t, j)),
                  gate_spec],
        out_specs=pl.BlockSpec((1, tm, tn), lambda j, b, t: (b, t, j)),
        out_shape=jax.ShapeDtypeStruct((bsz, n, nc), F32),
        compiler_params=_params(("arbitrary", "arbitrary", "arbitrary"), 48),
        name="matmul_resid",
    )(x, w, z, mods)


def _conv_kernel(x_ref, w_ref, b_ref, o_ref, ot_ref, *, n_ctx):
    x = x_ref[0]
    n = x.shape[0]
    row = lax.broadcasted_iota(jnp.int32, x.shape, 0)
    prev = jnp.where((row == 0) | (row == n_ctx), 0.0, pltpu.roll(x, 1, 0))
    nxt = jnp.where((row == n_ctx - 1) | (row == n - 1), 0.0, pltpu.roll(x, n - 1, 0))
    w = w_ref[...]
    y = b_ref[...] + prev * w[0:1] + x * w[1:2] + nxt * w[2:3]
    y = y * jax.nn.sigmoid(y)
    k_tiles = pl.num_programs(1) // 2
    y = y * jnp.where(pl.program_id(1) >= k_tiles, ML_DH ** -0.5, 1.0)
    o_ref[0] = y.astype(o_ref.dtype)
    ot_ref[0] = y.T.astype(ot_ref.dtype)


def conv_silu(proj, conv_w, conv_b, layer, n_ctx):
    bsz, n, _ = proj.shape
    tc = 256
    c2 = 2 * MIX_W
    return pl.pallas_call(
        functools.partial(_conv_kernel, n_ctx=n_ctx),
        grid=(bsz, c2 // tc),
        in_specs=[pl.BlockSpec((1, n, tc), lambda b, c: (b, 0, c)),
                  pl.BlockSpec((None, conv_w.shape[1], tc), lambda b, c: (layer, 0, c)),
                  pl.BlockSpec((None, 1, tc), lambda b, c: (layer, 0, c))],
        out_specs=[pl.BlockSpec((1, n, tc), lambda b, c: (b, 0, c)),
                   pl.BlockSpec((1, tc, n), lambda b, c: (b, c, 0))],
        out_shape=[jax.ShapeDtypeStruct((bsz, n, c2), BF16),
                   jax.ShapeDtypeStruct((bsz, c2, n), BF16)],
        compiler_params=_params(("arbitrary", "arbitrary"), 48),
        name="conv_silu",
    )(proj, conv_w, conv_b.reshape(conv_b.shape[0], 1, c2))


def _log_sigmoid(x):
    return jnp.minimum(x, 0.0) - jnp.log(1.0 + jnp.exp(-jnp.abs(x)))


def _mlstm_kernel(qf_ref, kf_ref, ktf_ref, vf_ref, gf_ref, gtf_ref,
                  qb_ref, kb_ref, ktb_ref, vb_ref, gb_ref, gtb_ref,
                  hf_ref, hb_ref, c_ref, m_ref):
    t = qf_ref.shape[1]
    dh = ML_DH

    @pl.when(pl.program_id(1) == 0)
    def _():
        c_ref[...] = jnp.zeros_like(c_ref)
        m_ref[...] = jnp.zeros_like(m_ref)

    r_i = lax.broadcasted_iota(jnp.int32, (t, t), 0)
    c_i = lax.broadcasted_iota(jnp.int32, (t, t), 1)
    low = c_i <= r_i
    upp = c_i >= r_i
    low_b = low.astype(BF16)
    upp_b = upp.astype(BF16)
    ones_col = (lax.broadcasted_iota(jnp.int32, (t, LANES), 1) == 0).astype(F32)

    dirs = ((qf_ref, kf_ref, ktf_ref, vf_ref, gf_ref, gtf_ref, hf_ref, low, low_b, upp_b),
            (qb_ref, kb_ref, ktb_ref, vb_ref, gb_ref, gtb_ref, hb_ref, upp, upp_b, low_b))
    for d, (q_ref, k_ref, kt_ref, v_ref, g_ref, gt_ref, h_ref, mask, col_tri, row_tri) in enumerate(dirs):
        g = g_ref[0]
        gt = gt_ref[0]
        lf = _log_sigmoid(g)
        lft = _log_sigmoid(gt)
        b_cols = sum(_dot(col_tri, p) for p in _split3(lf))
        b_rows = sum(_dot(p, row_tri) for p in _split3(lft))
        last = t - 1 if d == 0 else 0
        for hd in range(ML_HEADS):
            ci = 2 * d * ML_HEADS + hd
            cf = ci + ML_HEADS
            idx = d * ML_HEADS + hd
            sl = slice(hd * dh, (hd + 1) * dh)
            q = q_ref[0, :, sl]
            k = k_ref[0, :, sl]
            kt = kt_ref[0, sl, :]
            v = v_ref[0, :, sl]
            i_col = g[:, ci:ci + 1]
            i_row = gt[ci:ci + 1, :]
            b_col = b_cols[:, cf:cf + 1]
            b_row = b_rows[cf:cf + 1, :]
            b_tot = b_col[last:last + 1, :]
            m_prev = m_ref[idx][:, :1]
            c_prev = c_ref[idx]

            log_inter = b_col + m_prev
            log_intra = jnp.where(mask, b_col - b_row + i_row, -jnp.inf)
            m_t = jnp.maximum(log_inter, jnp.max(log_intra, axis=-1, keepdims=True))
            w_inter = jnp.exp(log_inter - m_t)
            s = _dot_nt(q, k) * jnp.exp(log_intra - m_t)
            v_ext = jnp.concatenate([v, ones_col], axis=1)
            nd = w_inter * _dot(q, c_prev.astype(BF16)) + _dot(s.astype(BF16), v_ext.astype(BF16))
            den = nd[:, dh:dh + 1]
            h_ref[0, :, sl] = nd[:, :dh] * (1.0 / jnp.maximum(jnp.abs(den), jnp.exp(-m_t)))

            log_w = b_tot - b_col + i_col
            m_new = jnp.maximum(b_tot + m_prev, jnp.max(log_w, axis=0, keepdims=True))
            decay = jnp.exp(b_tot + m_prev - m_new)
            w_s = jnp.exp(log_w - m_new)
            c_ref[idx] = decay * c_prev + _dot(kt, (v_ext * w_s).astype(BF16))
            m_ref[idx] = jnp.broadcast_to(m_new, (1, LANES))


def mlstm(qk, qkt, proj, gates, gates_t, n_ctx):
    bsz, n, _ = qk.shape
    t = SEQ_CHUNK
    nc = n // t
    c0 = n_ctx // t
    w = MIX_W
    fwd = lambda j: j
    bwd = lambda j: _rev_chunk(j, c0, nc)

    def specs(ch):
        return [pl.BlockSpec((1, t, w), lambda b, j: (b, ch(j), 0)),
                pl.BlockSpec((1, t, w), lambda b, j: (b, ch(j), 1)),
                pl.BlockSpec((1, w, t), lambda b, j: (b, 1, ch(j))),
                pl.BlockSpec((1, t, w), lambda b, j: (b, ch(j), COL_VA // w)),
                pl.BlockSpec((1, t, LANES), lambda b, j: (b, ch(j), 0)),
                pl.BlockSpec((1, N_GATES, t), lambda b, j: (b, 0, ch(j)))]

    args = [qk, qk, qkt, proj, gates, gates_t]
    out = jax.ShapeDtypeStruct((bsz, n, w), F32)
    return pl.pallas_call(
        _mlstm_kernel, grid=(bsz, nc),
        in_specs=specs(fwd) + specs(bwd),
        out_specs=[pl.BlockSpec((1, t, w), lambda b, j: (b, fwd(j), 0)),
                   pl.BlockSpec((1, t, w), lambda b, j: (b, bwd(j), 0))],
        out_shape=[out, out],
        scratch_shapes=[pltpu.VMEM((2 * ML_HEADS, ML_DH, ML_DH + LANES), F32),
                        pltpu.VMEM((2 * ML_HEADS, 1, LANES), F32)],
        compiler_params=_params(("arbitrary", "arbitrary"), 48),
        name="mlstm",
    )(*(args + args))


def _ya_kernel(hf_ref, hb_ref, oa_ref, g_ref, o_ref):
    h = hf_ref[0] + hb_ref[0]
    parts = []
    for hd in range(ML_HEADS):
        x = h[:, hd * ML_DH:(hd + 1) * ML_DH]
        parts.append(x * lax.rsqrt(jnp.mean(x * x, axis=-1, keepdims=True) + EPS))
    y = jnp.concatenate(parts, axis=1) * g_ref[...]
    o_ref[0] = (jax.nn.sigmoid(oa_ref[0]) * y).astype(o_ref.dtype)


def mlstm_out(hf, hb, proj, norm_g, layer):
    bsz, n, w = hf.shape
    tm = SEQ_CHUNK
    tile = pl.BlockSpec((1, tm, w), lambda b, t: (b, t, 0))
    return pl.pallas_call(
        _ya_kernel, grid=(bsz, n // tm),
        in_specs=[tile, tile, pl.BlockSpec((1, tm, w), lambda b, t: (b, t, COL_OA // w)),
                  pl.BlockSpec((None, 1, w), lambda b, t: (layer, 0, 0))],
        out_specs=tile, out_shape=jax.ShapeDtypeStruct((bsz, n, w), BF16),
        compiler_params=_params(("arbitrary", "arbitrary"), 40),
        name="mlstm_out",
    )(hf, hb, proj, norm_g.reshape(norm_g.shape[0], 1, w))


def _s5_kernel(uf_ref, ub_ref, bm_ref, cm_ref, a_ref, yf_ref, yb_ref, lhs_ref, re_ref, im_ref, y_ref, carry_ref):
    nb, t, _ = uf_ref.shape
    rows = t * nb
    sw = S5_BUNDLE * S5_STATE
    tiles_per_bundle = sw // LANES
    in_slabs = MIX_W // LANES

    @pl.when(pl.program_id(0) == 0)
    def _():
        carry_ref[...] = jnp.zeros_like(carry_ref)

    for d, u_ref in enumerate((uf_ref, ub_ref)):
        for b in range(nb):
            for s in range(in_slabs):
                lhs_ref[d, s, pl.ds(b, t, stride=nb), :] = u_ref[b, :, s * LANES:(s + 1) * LANES]
        for kb in range(S5_NB):
            lhs = jnp.concatenate([lhs_ref[d, 2 * kb], lhs_ref[d, 2 * kb + 1]], axis=1).astype(BF16)
            bu = _dot(lhs, bm_ref[d, kb])
            half = kb // (S5_NB // S5_HALVES)
            for lt in range(tiles_per_bundle):
                s = (kb % (S5_NB // S5_HALVES)) * tiles_per_bundle + lt
                re_ref[d, s, pl.ds(half, rows, stride=S5_HALVES), :] = bu[:, lt * LANES:(lt + 1) * LANES]
                im_ref[d, s, pl.ds(half, rows, stride=S5_HALVES), :] = bu[:, sw + lt * LANES:sw + (lt + 1) * LANES]

    for g in range(S5_SLABS // S5_GRP):
        slabs = tuple(range(g * S5_GRP, (g + 1) * S5_GRP))
        a_vals = [[(a_ref[d, 0, s], a_ref[d, 1, s]) for s in slabs] for d in range(2)]
        init = tuple(carry_ref[d, c, s] for d in range(2) for s in slabs for c in range(2))

        def body(i, xs, slabs=slabs, a_vals=a_vals):
            out = []
            for d in range(2):
                tok = i if d == 0 else t - 1 - i
                r0 = pl.multiple_of(tok * SUBLANES, SUBLANES)
                for j, s in enumerate(slabs):
                    xr, xi = xs[2 * (d * S5_GRP + j)], xs[2 * (d * S5_GRP + j) + 1]
                    ar, ai = a_vals[d][j]
                    nr = ar * xr - ai * xi + re_ref[d, s, pl.ds(r0, SUBLANES), :]
                    ni = ar * xi + ai * xr + im_ref[d, s, pl.ds(r0, SUBLANES), :]
                    re_ref[d, s, pl.ds(r0, SUBLANES), :] = nr
                    im_ref[d, s, pl.ds(r0, SUBLANES), :] = ni
                    out += [nr, ni]
            return tuple(out)

        fin = lax.fori_loop(0, t, body, init, unroll=2)
        for d in range(2):
            for j, s in enumerate(slabs):
                for c in range(2):
                    carry_ref[d, c, s] = fin[2 * (d * S5_GRP + j) + c]

    for d, out_ref in enumerate((yf_ref, yb_ref)):
        for kb in range(S5_NB):
            half = kb // (S5_NB // S5_HALVES)
            s0 = (kb % (S5_NB // S5_HALVES)) * tiles_per_bundle
            sr = jnp.concatenate([re_ref[d, s0 + lt, pl.ds(half, rows, stride=S5_HALVES), :]
                                  for lt in range(tiles_per_bundle)], axis=1).astype(BF16)
            si = jnp.concatenate([im_ref[d, s0 + lt, pl.ds(half, rows, stride=S5_HALVES), :]
                                  for lt in range(tiles_per_bundle)], axis=1).astype(BF16)
            y = _dot(sr, cm_ref[kb, 0]) + _dot(si, cm_ref[kb, 1])
            y_ref[d, 2 * kb] = y[:, :LANES]
            y_ref[d, 2 * kb + 1] = y[:, LANES:]
        for b in range(nb):
            for s in range(in_slabs):
                out_ref[b, :, s * LANES:(s + 1) * LANES] = y_ref[d, s, pl.ds(b, t, stride=nb), :]


def s5_scan(proj, bmat, cmat, atab, n_ctx):
    bsz, n, _ = proj.shape
    assert bsz * S5_HALVES == SUBLANES, "the scan layout puts (sample, state half) on the 8 sublanes"
    t = S5_TC
    nc = n // t
    c0 = n_ctx // t
    w = MIX_W
    rows = t * bsz
    bwd = lambda j: _rev_chunk(j, c0, nc)
    out = jax.ShapeDtypeStruct((bsz, n, w), F32)
    const = lambda a: pl.BlockSpec(a.shape, lambda j: (0,) * a.ndim)
    return pl.pallas_call(
        _s5_kernel, grid=(nc,),
        in_specs=[pl.BlockSpec((bsz, t, w), lambda j: (0, j, COL_US // w)),
                  pl.BlockSpec((bsz, t, w), lambda j: (0, bwd(j), COL_US // w)),
                  const(bmat), const(cmat), const(atab)],
        out_specs=[pl.BlockSpec((bsz, t, w), lambda j: (0, j, 0)),
                   pl.BlockSpec((bsz, t, w), lambda j: (0, bwd(j), 0))],
        out_shape=[out, out],
        scratch_shapes=[pltpu.VMEM((2, w // LANES, rows, LANES), F32),
                        pltpu.VMEM((2, S5_SLABS, rows * S5_HALVES, LANES), F32),
                        pltpu.VMEM((2, S5_SLABS, rows * S5_HALVES, LANES), F32),
                        pltpu.VMEM((2, w // LANES, rows, LANES), F32),
                        pltpu.VMEM((2, 2, S5_SLABS, SUBLANES, LANES), F32)],
        compiler_params=_params(("arbitrary",), 56),
        name="s5_scan",
    )(proj, proj, bmat, cmat, atab)


def s5_operands(lam_re, lam_im, log_dt, b_re, b_im, c_re, c_im):
    lam_re, lam_im = lam_re.astype(F32), lam_im.astype(F32)
    dt = jnp.exp(log_dt.astype(F32))[..., None]
    mag = jnp.exp(lam_re * dt)
    a_re, a_im = mag * jnp.cos(lam_im * dt), mag * jnp.sin(lam_im * dt)
    nr, ni = a_re - 1.0, a_im
    den = lam_re * lam_re + lam_im * lam_im
    fr = (nr * lam_re + ni * lam_im) / den
    fi = (ni * lam_re - nr * lam_im) / den
    bb_re = fr[..., None] * b_re - fi[..., None] * b_im
    bb_im = fr[..., None] * b_im + fi[..., None] * b_re
    eye = jnp.eye(S5_BUNDLE, dtype=F32)

    def blockdiag_b(bb):
        x = bb.reshape(2, S5_NB, S5_BUNDLE, S5_STATE, S5_GROUP)
        return jnp.einsum('dkgpi,gh->dkgihp', x, eye).reshape(
            2, S5_NB, S5_BUNDLE * S5_GROUP, S5_BUNDLE * S5_STATE)

    bmat = jnp.concatenate([blockdiag_b(bb_re), blockdiag_b(bb_im)], -1).astype(BF16)

    def blockdiag_c(c):
        x = c.astype(F32).reshape(S5_NB, S5_BUNDLE, S5_GROUP, S5_STATE)
        return jnp.einsum('kgip,gh->kgphi', x, eye).reshape(
            S5_NB, S5_BUNDLE * S5_STATE, S5_BUNDLE * S5_GROUP)

    cmat = jnp.stack([blockdiag_c(c_re), -blockdiag_c(c_im)], 1).astype(BF16)

    def slab_table(a):
        a4 = a.reshape(2, S5_HALVES, S5_SLABS, LANES)
        return jnp.take(a4, jnp.arange(SUBLANES) % S5_HALVES, axis=1).transpose(0, 2, 1, 3)

    atab = jnp.stack([slab_table(a_re), slab_table(a_im)], 1)
    return bmat, cmat, atab


def _glu_kernel(yf_ref, yb_ref, u_ref, d_ref, w_ref, b_ref, o_ref):
    y = d_ref[...] * u_ref[0] + yf_ref[0] + yb_ref[0]
    z = jax.nn.gelu(y)
    o_ref[0] = (z * jax.nn.sigmoid(_dot(z.astype(BF16), w_ref[...].astype(BF16)) + b_ref[...])).astype(o_ref.dtype)


def s5_glu(yf, yb, proj, d_skip, glu_w, glu_b, layer):
    bsz, n, w = yf.shape
    tm = SEQ_CHUNK
    tile = pl.BlockSpec((1, tm, w), lambda b, t: (b, t, 0))
    vec = pl.BlockSpec((None, 1, w), lambda b, t: (layer, 0, 0))
    return pl.pallas_call(
        _glu_kernel, grid=(bsz, n // tm),
        in_specs=[tile, tile, pl.BlockSpec((1, tm, w), lambda b, t: (b, t, COL_US // w)),
                  vec, pl.BlockSpec((None, w, w), lambda b, t: (layer, 0, 0)), vec],
        out_specs=tile, out_shape=jax.ShapeDtypeStruct((bsz, n, w), BF16),
        compiler_params=_params(("arbitrary", "arbitrary"), 40),
        name="s5_glu",
    )(yf, yb, proj, d_skip.reshape(-1, 1, w), glu_w, glu_b.reshape(-1, 1, w))


def _da_prep_kernel(x_ref, g_ref, cos_ref, sin_ref, o_ref):
    cos = cos_ref[...]
    sin = sin_ref[...]
    lane = lax.broadcasted_iota(jnp.int32, cos.shape, 1)
    first = lane < DA_DQK
    quarter = DA_DQK // 4
    up = (lane & quarter) == 0
    scale = jnp.where(pl.program_id(2) == 0, DA_DQK ** -0.5, 1.0)
    for hd in range(DA_HEADS):
        sl = slice(hd * LANES, (hd + 1) * LANES)
        x = x_ref[0, :, sl]
        x2 = x * x
        ms0 = jnp.sum(jnp.where(first, x2, 0.0), axis=-1, keepdims=True)
        ms1 = jnp.sum(jnp.where(first, 0.0, x2), axis=-1, keepdims=True)
        ms = jnp.where(first, ms0, ms1) * (1.0 / DA_DQK)
        y = x * lax.rsqrt(ms + EPS) * g_ref[:, sl]
        partner = jnp.where(up, pltpu.roll(y, LANES - quarter, 1), pltpu.roll(y, quarter, 1))
        o_ref[0, :, sl] = ((y * cos + partner * sin) * scale).astype(o_ref.dtype)


def da_prep(proj, qn_g, kn_g, cos_t, sin_t, layer):
    bsz, n, _ = proj.shape
    tm = SEQ_CHUNK
    w = DA_HEADS * LANES
    reps = w // DA_DQK
    gains = jnp.concatenate([jnp.tile(qn_g, (1, reps)), jnp.tile(kn_g, (1, reps))], -1)
    gains = gains.reshape(gains.shape[0], 1, 2 * w)
    return pl.pallas_call(
        _da_prep_kernel, grid=(bsz, n // tm, 2),
        in_specs=[pl.BlockSpec((1, tm, w), lambda b, t, j: (b, t, COL_QD // w + j)),
                  pl.BlockSpec((None, 1, w), lambda b, t, j: (layer, 0, j)),
                  pl.BlockSpec((tm, LANES), lambda b, t, j: (t, 0)),
                  pl.BlockSpec((tm, LANES), lambda b, t, j: (t, 0))],
        out_specs=pl.BlockSpec((1, tm, w), lambda b, t, j: (b, t, j)),
        out_shape=jax.ShapeDtypeStruct((bsz, n, 2 * w), BF16),
        compiler_params=_params(("arbitrary", "arbitrary", "arbitrary"), 40),
        name="da_prep",
    )(proj, gains, cos_t, sin_t)


def rope_tables(n, n_ctx):
    lat = jnp.arange(n) - n_ctx
    rows = (lat // GRID_W).astype(F32)
    cols = (lat % GRID_W).astype(F32)
    lane = jnp.arange(LANES)
    in_comp = lane % DA_DQK
    quarter = DA_DQK // 4
    freqs = ROPE_BASE ** (-(in_comp % quarter).astype(F32) / quarter)
    pos = jnp.where((in_comp < DA_DQK // 2)[None, :], rows[:, None], cols[:, None])
    ang = pos * freqs[None, :]
    sign = jnp.where((in_comp % (2 * quarter)) < quarter, -1.0, 1.0)
    is_lat = (lat >= 0)[:, None]
    cos_t = jnp.where(is_lat, jnp.cos(ang), 1.0)
    sin_t = jnp.where(is_lat, jnp.sin(ang) * sign[None, :], 0.0)
    return cos_t.astype(F32), sin_t.astype(F32)


def _softmax_rows(s):
    e = jnp.exp(s - jnp.max(s, axis=-1, keepdims=True))
    return e * (1.0 / jnp.sum(e, axis=-1, keepdims=True))


def _diff_attn_kernel(scal_ref, q_ref, k_ref, v_ref, g_ref, o_ref, *, n_ctx, ctx_tiles):
    lam = scal_ref[0]
    out_scale = scal_ref[1]
    q = q_ref[0]
    lane = lax.broadcasted_iota(jnp.int32, q.shape, 1)
    zero = jnp.zeros_like(q)
    q0 = jnp.where(lane < DA_DQK, q, zero)
    q1 = jnp.where(lane < DA_DQK, zero, q)

    def attend(k, v):
        p0 = _softmax_rows(_dot_nt(q0, k))
        p1 = _softmax_rows(_dot_nt(q1, k))
        o = _dot((p0 - lam * p1).astype(BF16), v.astype(BF16))
        o = o * lax.rsqrt(jnp.mean(o * o, axis=-1, keepdims=True) + EPS) * g_ref[...]
        o_ref[0] = (o * out_scale).astype(o_ref.dtype)

    @pl.when(pl.program_id(2) < ctx_tiles)
    def _():
        attend(k_ref[0, :n_ctx, :], v_ref[0, :n_ctx, :])

    @pl.when(pl.program_id(2) >= ctx_tiles)
    def _():
        attend(k_ref[0], v_ref[0])


def diff_attn(qk_hat, proj, scal, sub_g, layer, n_ctx):
    bsz, n, _ = qk_hat.shape
    tq = SEQ_CHUNK
    return pl.pallas_call(
        functools.partial(_diff_attn_kernel, n_ctx=n_ctx, ctx_tiles=n_ctx // tq),
        grid=(bsz, DA_HEADS, n // tq),
        in_specs=[pl.BlockSpec(memory_space=pltpu.SMEM),
                  pl.BlockSpec((1, tq, LANES), lambda b, h, t: (b, t, h)),
                  pl.BlockSpec((1, n, LANES), lambda b, h, t: (b, 0, DA_HEADS + h)),
                  pl.BlockSpec((1, n, LANES), lambda b, h, t: (b, 0, COL_VD // LANES + h)),
                  pl.BlockSpec((None, 1, LANES), lambda b, h, t: (layer, 0, 0))],
        out_specs=pl.BlockSpec((1, tq, LANES), lambda b, h, t: (b, t, h)),
        out_shape=jax.ShapeDtypeStruct((bsz, n, DA_HEADS * DA_DV), BF16),
        compiler_params=_params(("arbitrary", "arbitrary", "arbitrary"), 48),
        name="diff_attn",
    )(scal, qk_hat, qk_hat, proj, sub_g.reshape(sub_g.shape[0], 1, DA_DV))


def _merge_kernel(ya_ref, yb_ref, yc_ref, p_ref, ga_ref, gb_ref, gc_ref, o_ref):
    acc = _dot(ya_ref[0], p_ref[0].astype(BF16)) * jax.nn.sigmoid(ga_ref[0])
    acc += _dot(yb_ref[0], p_ref[1].astype(BF16)) * jax.nn.sigmoid(gb_ref[0])
    acc += _dot(yc_ref[0], p_ref[2].astype(BF16)) * jax.nn.sigmoid(gc_ref[0])
    o_ref[0] = acc.astype(o_ref.dtype)


def branch_merge(ya, yb, yc, proj, branch_proj, layer):
    bsz, n, w = ya.shape
    d = branch_proj.shape[-1]
    tm, tn = 768, 512
    y_tile = pl.BlockSpec((1, tm, w), lambda j, b, t: (b, t, 0))
    gate = lambda r: pl.BlockSpec((1, tm, tn), lambda j, b, t: (b, t, (COL_GB + r * d) // tn + j))
    return pl.pallas_call(
        _merge_kernel, grid=(d // tn, bsz, n // tm),
        in_specs=[y_tile, y_tile, y_tile,
                  pl.BlockSpec((None, N_BRANCH, w, tn), lambda j, b, t: (layer, 0, 0, j)),
                  gate(0), gate(1), gate(2)],
        out_specs=pl.BlockSpec((1, tm, tn), lambda j, b, t: (b, t, j)),
        out_shape=jax.ShapeDtypeStruct((bsz, n, d), BF16),
        compiler_params=_params(("arbitrary", "arbitrary", "arbitrary"), 48),
        name="branch_merge",
    )(ya, yb, yc, branch_proj, proj, proj, proj)


INFO_E, INFO_POS, INFO_W = 0, 2, 4


def _router_kernel(h_ref, w_ref, b_ref, info_ref, cnt_ref):
    tm = h_ref.shape[1]

    @pl.when((pl.program_id(0) == 0) & (pl.program_id(1) == 0))
    def _():
        cnt_ref[...] = jnp.zeros_like(cnt_ref)

    h_parts = _split3(h_ref[0])
    w_parts = _split3(w_ref[...])
    logits = (_dot_nt(w_parts[0], h_parts[0]) + _dot_nt(w_parts[0], h_parts[1])
              + _dot_nt(w_parts[1], h_parts[0]))
    score = jax.nn.sigmoid(logits)
    sel = score + b_ref[...]
    rows = [sel[e:e + 1, :] for e in range(N_EXPERTS)]
    gscores = []
    for g in range(N_EXPERT_GROUPS):
        a, b, c, d = rows[g * EXPERTS_PER_GROUP:(g + 1) * EXPERTS_PER_GROUP]
        hi1, lo1 = jnp.maximum(a, b), jnp.minimum(a, b)
        hi2, lo2 = jnp.maximum(c, d), jnp.minimum(c, d)
        gscores.append(jnp.maximum(hi1, hi2) + jnp.maximum(jnp.minimum(hi1, hi2), jnp.maximum(lo1, lo2)))
    gmax = functools.reduce(jnp.maximum, gscores)
    taken = jnp.zeros_like(gmax, dtype=jnp.bool_)
    picked = []
    for g in range(N_EXPERT_GROUPS):
        is_g = (gscores[g] == gmax) & jnp.logical_not(taken)
        picked.append(is_g)
        taken = taken | is_g
    chosen = []
    for e in range(N_EXPERTS):
        g = e // EXPERTS_PER_GROUP
        rank = jnp.zeros_like(gmax)
        for o in range(g * EXPERTS_PER_GROUP, (g + 1) * EXPERTS_PER_GROUP):
            if o == e:
                continue
            ahead = (rows[o] > rows[e]) | ((rows[o] == rows[e]) & (o < e))
            rank = rank + ahead.astype(F32)
        chosen.append(picked[g] & (rank < float(TOP_K)))
    onehot = jnp.concatenate([c.astype(F32) for c in chosen], axis=0)
    total = jnp.sum(onehot * score, axis=0, keepdims=True)
    gates = onehot * score * (1.0 / total)

    r_i = lax.broadcasted_iota(jnp.int32, (tm, tm), 0)
    c_i = lax.broadcasted_iota(jnp.int32, (tm, tm), 1)
    incl = _dot(onehot.astype(BF16), (r_i <= c_i).astype(BF16))
    pos = incl - onehot + cnt_ref[:, :1]
    cnt_ref[...] = cnt_ref[...] + incl[:, tm - 1:tm]

    seen = jnp.zeros_like(gmax)
    rec = [jnp.zeros_like(gmax) for _ in range(3 * TOP_K)]
    for e in range(N_EXPERTS):
        oh = onehot[e:e + 1, :]
        for k, is_k in enumerate((oh * (1.0 - seen), oh * seen)):
            rec[INFO_E + k] += is_k * float(e)
            rec[INFO_POS + k] += is_k * pos[e:e + 1, :]
            rec[INFO_W + k] += is_k * gates[e:e + 1, :]
        seen = seen + oh
    pad = jnp.zeros((LANES - len(rec), tm), F32)
    info_ref[0] = jnp.concatenate(rec + [pad], axis=0).T


def moe_router(h2, router_w, router_b):
    bsz, n, d = h2.shape
    tm = SEQ_CHUNK
    return pl.pallas_call(
        _router_kernel, grid=(bsz, n // tm),
        in_specs=[pl.BlockSpec((1, tm, d), lambda b, t: (b, t, 0)),
                  pl.BlockSpec((N_EXPERTS, d), lambda b, t: (0, 0)),
                  pl.BlockSpec((N_EXPERTS, 1), lambda b, t: (0, 0))],
        out_specs=[pl.BlockSpec((1, tm, LANES), lambda b, t: (b, t, 0)),
                   pl.BlockSpec((N_EXPERTS, LANES), lambda b, t: (0, 0))],
        out_shape=[jax.ShapeDtypeStruct((bsz, n, LANES), F32),
                   jax.ShapeDtypeStruct((N_EXPERTS, LANES), F32)],
        compiler_params=_params(("arbitrary", "arbitrary"), 40),
        name="moe_router",
    )(h2, router_w.T, router_b.reshape(N_EXPERTS, 1))


def moe_plan(info, counts, n_tiles):
    m = info.shape[0] * info.shape[1]
    rec = info.reshape(m, LANES)
    cnt = counts[:, 0].astype(jnp.int32)
    padded = (cnt + MOE_TILE - 1) // MOE_TILE * MOE_TILE
    ends = jnp.cumsum(padded)
    starts = ends - padded
    experts = rec[:, INFO_E:INFO_E + TOP_K].astype(jnp.int32)
    dest = starts[experts] + rec[:, INFO_POS:INFO_POS + TOP_K].astype(jnp.int32)
    tile_start = jnp.arange(n_tiles, dtype=jnp.int32) * MOE_TILE
    tile_expert = jnp.minimum(jnp.searchsorted(ends, tile_start, side='right'), N_EXPERTS - 1).astype(jnp.int32)
    n_used = (ends[-1] // MOE_TILE).astype(jnp.int32).reshape(1)
    return dest.reshape(-1), tile_expert, n_used


def _row_copy(src_ref, src_row, dst_ref, dst_row, sem):
    return pltpu.make_async_copy(src_ref.at[pl.ds(src_row, 1), :], dst_ref.at[pl.ds(dst_row, 1), :], sem)


def _dispatch_kernel(dest_ref, h_ref, xs_in_ref, xs_ref, sem):
    del xs_in_ref
    tm = h_ref.shape[0]
    base = pl.program_id(0) * tm

    def start(r, carry):
        for k in range(TOP_K):
            _row_copy(h_ref, r, xs_ref, dest_ref[(base + r) * TOP_K + k], sem).start()
        return carry

    def wait(r, carry):
        for k in range(TOP_K):
            _row_copy(h_ref, 0, xs_ref, 0, sem).wait()
        return carry

    lax.fori_loop(0, tm, start, 0)
    lax.fori_loop(0, tm, wait, 0)


def moe_dispatch(h2, dest, n_rows):
    m, d = h2.shape
    tm = SEQ_CHUNK
    grid_spec = pltpu.PrefetchScalarGridSpec(
        num_scalar_prefetch=1, grid=(m // tm,),
        in_specs=[pl.BlockSpec((tm, d), lambda i, dest: (i, 0)),
                  pl.BlockSpec(memory_space=pl.ANY)],
        out_specs=pl.BlockSpec(memory_space=pl.ANY),
        scratch_shapes=[pltpu.SemaphoreType.DMA])
    return pl.pallas_call(
        _dispatch_kernel, grid_spec=grid_spec,
        out_shape=jax.ShapeDtypeStruct((n_rows, d), h2.dtype),
        input_output_aliases={2: 0},
        compiler_params=_params(("arbitrary",), 40),
        name="moe_dispatch",
    )(dest, h2, jnp.zeros((n_rows, d), h2.dtype))


def _expert_kernel(te_ref, nu_ref, x_ref, wg_ref, wu_ref, wd_ref, o_ref):
    del te_ref

    @pl.when(pl.program_id(0) < nu_ref[0])
    def _():
        x = x_ref[...].astype(BF16)
        a = _dot(x, wg_ref[...].astype(BF16))
        act = (a * jax.nn.sigmoid(a)) * _dot(x, wu_ref[...].astype(BF16))
        o_ref[...] = _dot(act.astype(BF16), wd_ref[...].astype(BF16))

    @pl.when(pl.program_id(0) >= nu_ref[0])
    def _():
        o_ref[...] = jnp.zeros_like(o_ref)


def moe_experts(xs, tile_expert, n_used, w_gate, w_up, w_down, layer):
    n_rows, d = xs.shape
    f = w_gate.shape[-1]
    w_spec = lambda r, c: pl.BlockSpec((None, None, r, c), lambda i, te, nu: (layer, te[i], 0, 0))
    grid_spec = pltpu.PrefetchScalarGridSpec(
        num_scalar_prefetch=2, grid=(n_rows // MOE_TILE,),
        in_specs=[pl.BlockSpec((MOE_TILE, d), lambda i, te, nu: (i, 0)),
                  w_spec(d, f), w_spec(d, f), w_spec(f, d)],
        out_specs=pl.BlockSpec((MOE_TILE, d), lambda i, te, nu: (i, 0)))
    return pl.pallas_call(
        _expert_kernel, grid_spec=grid_spec,
        out_shape=jax.ShapeDtypeStruct((n_rows, d), F32),
        compiler_params=_params(("arbitrary",), 56),
        name="moe_experts",
    )(tile_expert, n_used, xs, w_gate, w_up, w_down)


def _combine_kernel(dest_ref, ys_ref, info_ref, z_ref, gate_ref, o_ref, buf_ref, sem):
    tm = z_ref.shape[1]
    base = (pl.program_id(0) * pl.num_programs(1) + pl.program_id(1)) * tm

    def start(r, carry):
        for k in range(TOP_K):
            _row_copy(ys_ref, dest_ref[(base + r) * TOP_K + k], buf_ref.at[k], r, sem).start()
        return carry

    def wait(r, carry):
        for k in range(TOP_K):
            _row_copy(ys_ref, 0, buf_ref.at[k], 0, sem).wait()
        return carry

    lax.fori_loop(0, tm, start, 0)
    lax.fori_loop(0, tm, wait, 0)
    info = info_ref[0]
    mix = info[:, INFO_W:INFO_W + 1] * buf_ref[0] + info[:, INFO_W + 1:INFO_W + 2] * buf_ref[1]
    o_ref[0] = z_ref[0] + gate_ref[...] * mix


def moe_combine(ys, dest, info, z, mods, layer, gate_chunk, n_ctx):
    bsz, n, d = z.shape
    tm = SEQ_CHUNK
    ctx_tiles = n_ctx // tm
    tile = pl.BlockSpec((1, tm, d), lambda b, t, dest: (b, t, 0))
    grid_spec = pltpu.PrefetchScalarGridSpec(
        num_scalar_prefetch=1, grid=(bsz, n // tm),
        in_specs=[pl.BlockSpec(memory_space=pl.ANY),
                  pl.BlockSpec((1, tm, LANES), lambda b, t, dest: (b, t, 0)),
                  tile,
                  pl.BlockSpec((None, None, 1, d),
                               lambda b, t, dest: (layer, jnp.where(t < ctx_tiles, bsz, b), 0, gate_chunk))],
        out_specs=tile,
        scratch_shapes=[pltpu.VMEM((TOP_K, tm, d), F32), pltpu.SemaphoreType.DMA])
    return pl.pallas_call(
        _combine_kernel, grid_spec=grid_spec,
        out_shape=jax.ShapeDtypeStruct(z.shape, F32),
        compiler_params=_params(("arbitrary", "arbitrary"), 40),
        name="moe_combine",
    )(dest, ys, info, z, mods)


def kernel(x, c, ctx, c_ctx, norm1_g, norm2_g, ada_w, ada_b, w_in, b_in, ml_conv_w, ml_conv_b, ml_norm_g, s5_lam_re, s5_lam_im, s5_log_dt, s5_b_re, s5_b_im, s5_c_re, s5_c_im, s5_d, s5_glu_w, s5_glu_b, da_q_norm_g, da_k_norm_g, da_lam_q1, da_lam_k1, da_lam_q2, da_lam_k2, da_sub_norm_g, branch_proj, w_out, router_w, router_b, exp_w_gate, exp_w_up, exp_w_down):
    bsz, n_lat, d = x.shape
    n_ctx = ctx.shape[1]
    n = n_ctx + n_lat
    depth = w_in.shape[0]
    assert n_ctx % SEQ_CHUNK == 0 and n_lat % SEQ_CHUNK == 0 and d == D_MODEL

    rows = -(-(bsz + 1) // SUBLANES) * SUBLANES
    cvec = jnp.zeros((rows, d), F32).at[:bsz].set(c).at[bsz].set(c_ctx)
    mods = ada_mod(cvec, ada_w, ada_b).reshape(depth, rows, 1, 6 * d)

    w_main = jnp.concatenate([w_in[..., :GATE_OFF], w_in[..., GATE_OFF + N_GATES:]], -1).astype(BF16)
    b_main = jnp.concatenate([b_in[..., :GATE_OFF], b_in[..., GATE_OFF + N_GATES:]], -1)[:, None, :]
    w_gates = jnp.pad(w_in[..., GATE_OFF:GATE_OFF + N_GATES], ((0, 0), (0, 0), (0, LANES - N_GATES))).astype(BF16)
    b_gates = jnp.pad(b_in[..., GATE_OFF:GATE_OFF + N_GATES], ((0, 0), (0, LANES - N_GATES)))[:, None, :]

    m_tok = bsz * n
    moe_tiles = m_tok * TOP_K // MOE_TILE + N_EXPERTS
    cos_t, sin_t = rope_tables(n, n_ctx)
    z = jnp.concatenate([ctx, x], 1)
    for i in range(depth):
        lam_init = 0.8 - 0.6 * math.exp(-0.3 * i)
        (h,) = norm_mod(z, norm1_g, mods, i, 0, 1, n_ctx, (BF16,))
        proj = matmul_bias(h, w_main, b_main, i, n // 2, 1024)
        gates = matmul_bias(h, w_gates, b_gates, i, n // 2, LANES)
        gates_t = jnp.swapaxes(gates[..., :N_GATES], 1, 2)

        qk, qkt = conv_silu(proj, ml_conv_w, ml_conv_b, i, n_ctx)
        hf, hb = mlstm(qk, qkt, proj, gates, gates_t, n_ctx)
        ya = mlstm_out(hf, hb, proj, ml_norm_g, i)

        bmat, cmat, atab = s5_operands(s5_lam_re[i], s5_lam_im[i], s5_log_dt[i], s5_b_re[i], s5_b_im[i],
                                       s5_c_re[i], s5_c_im[i])
        yf, ybk = s5_scan(proj, bmat, cmat, atab, n_ctx)
        yb = s5_glu(yf, ybk, proj, s5_d, s5_glu_w, s5_glu_b, i)

        qk_hat = da_prep(proj, da_q_norm_g, da_k_norm_g, cos_t, sin_t, i)
        lam = (jnp.exp(jnp.sum(da_lam_q1[i].astype(F32) * da_lam_k1[i].astype(F32)))
               - jnp.exp(jnp.sum(da_lam_q2[i].astype(F32) * da_lam_k2[i].astype(F32))) + lam_init)
        scal = jnp.stack([lam, jnp.asarray(1.0 - lam_init, F32)]).astype(F32)
        yc = diff_attn(qk_hat, proj, scal, da_sub_norm_g, i, n_ctx)

        merged = branch_merge(ya, yb, yc, proj, branch_proj, i)
        z = matmul_resid(merged, w_out, z, mods, i, 2, n_ctx)

        (h2,) = norm_mod(z, norm2_g, mods, i, 3, 4, n_ctx, (F32,))
        info, counts = moe_router(h2, router_w, router_b)
        dest, tile_expert, n_used = moe_plan(info, counts, moe_tiles)
        xs = moe_dispatch(h2.reshape(m_tok, d), dest, moe_tiles * MOE_TILE)
        ys = moe_experts(xs, tile_expert, n_used, exp_w_gate, exp_w_up, exp_w_down, i)
        z = moe_combine(ys, dest, info, z, mods, i, 5, n_ctx)
    return z[:, n_ctx:]
```

```python
import functools
import math

import jax
import jax.numpy as jnp
from jax import lax
from jax.experimental import pallas as pl
from jax.experimental.pallas import tpu as pltpu

F32 = jnp.float32
BF16 = jnp.bfloat16

D_MODEL = 2048
GRID_W = 64
MIX_W = 1024
N_BRANCH = 3
ML_HEADS = 4
ML_DH = MIX_W // ML_HEADS
S5_GROUP = 16
S5_GROUPS = MIX_W // S5_GROUP
S5_STATE = 64
S5_BUNDLE = 16
S5_NB = S5_GROUPS // S5_BUNDLE
S5_NS = S5_GROUPS * S5_STATE
DA_HEADS = 8
DA_DQK = 64
DA_DV = 2 * DA_DQK
ROPE_BASE = 10000.0
N_EXPERTS = 16
N_EXPERT_GROUPS = 4
EXPERTS_PER_GROUP = N_EXPERTS // N_EXPERT_GROUPS
TOP_K = 2
D_FF_EXPERT = 512
EPS = 1e-6

COL_QA, COL_KA, COL_VA, COL_OA = 0, 1024, 2048, 3072
COL_US, COL_QD, COL_KD, COL_VD, COL_GB = 0, 1024, 2048, 3072, 4096
GATE_OFF = 4 * MIX_W
N_GATES = 4 * ML_HEADS
REST_OFF = GATE_OFF + N_GATES
REST_COLS = 4 * MIX_W + N_BRANCH * D_MODEL

SEQ_CHUNK = 256
LANES = 128
SUBLANES = 8
MIB = 1024 * 1024

S5_TC = 64
S5_HALVES = 2
S5_SLABS = S5_NS // S5_HALVES // LANES
S5_GRP = 4
MOE_TILE = 256
DMA_UNROLL = 8


def _params(sem, vmem_mib):
    return pltpu.CompilerParams(dimension_semantics=sem, vmem_limit_bytes=vmem_mib * MIB)


def _dot(a, b):
    return jnp.dot(a, b, preferred_element_type=F32)


def _dot_nt(a, b):
    return lax.dot_general(a, b, (((1,), (1,)), ((), ())), preferred_element_type=F32)


def _split3(x):
    hi = x.astype(BF16)
    r = x - hi.astype(F32)
    mid = r.astype(BF16)
    lo = (r - mid.astype(F32)).astype(BF16)
    return hi, mid, lo


def _rev_chunk(j, n_ctx_chunks, n_chunks):
    return jnp.where(j < n_ctx_chunks, n_ctx_chunks - 1 - j, n_chunks - 1 - (j - n_ctx_chunks))


def _ada_kernel(c_ref, w_ref, b_ref, o_ref):
    c = c_ref[...]
    s = (c * jax.nn.sigmoid(c)).astype(BF16)
    o_ref[0] = _dot(s, w_ref[0].astype(BF16)) + b_ref[0]


def ada_mod(cvec, ada_w, ada_b):
    depth, d, n6 = ada_w.shape
    rows = cvec.shape[0]
    tn = 1024
    return pl.pallas_call(
        _ada_kernel,
        grid=(depth, n6 // tn),
        in_specs=[pl.BlockSpec((rows, d), lambda l, j: (0, 0)),
                  pl.BlockSpec((1, d, tn), lambda l, j: (l, 0, j)),
                  pl.BlockSpec((1, 1, tn), lambda l, j: (l, 0, j))],
        out_specs=pl.BlockSpec((1, rows, tn), lambda l, j: (l, 0, j)),
        out_shape=jax.ShapeDtypeStruct((depth, rows, n6), F32),
        compiler_params=_params(("arbitrary", "arbitrary"), 40),
        name="ada_mod",
    )(cvec, ada_w, ada_b.reshape(depth, 1, n6))


def _mod_spec(layer, chunk, batch, ctx_tiles):
    return pl.BlockSpec((None, None, 1, D_MODEL),
                        lambda b, t: (layer, jnp.where(t < ctx_tiles, batch, b), 0, chunk))


def _norm_mod_kernel(z_ref, g_ref, shift_ref, scale_ref, *outs):
    z = z_ref[0]
    y = z * lax.rsqrt(jnp.mean(z * z, axis=-1, keepdims=True) + EPS) * g_ref[...]
    h = y * (1.0 + scale_ref[...]) + shift_ref[...]
    for o in outs:
        o[0] = h.astype(o.dtype)


def norm_mod(z, norm_g, mods, layer, shift_chunk, scale_chunk, n_ctx, out_dtypes):
    bsz, n, d = z.shape
    tm = SEQ_CHUNK
    ctx_tiles = n_ctx // tm
    tile = pl.BlockSpec((1, tm, d), lambda b, t: (b, t, 0))
    return pl.pallas_call(
        _norm_mod_kernel, grid=(bsz, n // tm),
        in_specs=[tile, pl.BlockSpec((None, 1, d), lambda b, t: (layer, 0, 0)),
                  _mod_spec(layer, shift_chunk, bsz, ctx_tiles), _mod_spec(layer, scale_chunk, bsz, ctx_tiles)],
        out_specs=[tile] * len(out_dtypes),
        out_shape=[jax.ShapeDtypeStruct(z.shape, dt) for dt in out_dtypes],
        compiler_params=_params(("arbitrary", "arbitrary"), 40),
        name="norm_mod",
    )(z, norm_g.reshape(norm_g.shape[0], 1, d), mods, mods)


def _mm_kernel(x_ref, w_ref, b_ref, o_ref):
    acc = _dot(x_ref[0].astype(BF16), w_ref[...].astype(BF16)) + b_ref[...]
    o_ref[0] = acc.astype(o_ref.dtype)


def matmul_bias(x, w, bias, layer, tm, tn, col0, n_cols, out_dtype=F32):
    bsz, n, k = x.shape
    cb = col0 // tn
    return pl.pallas_call(
        _mm_kernel, grid=(n_cols // tn, bsz, n // tm),
        in_specs=[pl.BlockSpec((1, tm, k), lambda j, b, t: (b, t, 0)),
                  pl.BlockSpec((None, k, tn), lambda j, b, t: (layer, 0, cb + j)),
                  pl.BlockSpec((None, 1, tn), lambda j, b, t: (layer, 0, cb + j))],
        out_specs=pl.BlockSpec((1, tm, tn), lambda j, b, t: (b, t, j)),
        out_shape=jax.ShapeDtypeStruct((bsz, n, n_cols), out_dtype),
        compiler_params=_params(("arbitrary", "arbitrary", "arbitrary"), 48),
        name="matmul_bias",
    )(x, w, bias)


def _in_proj_kernel(x_ref, wa_ref, wn_ref, ba_ref, bn_ref, o_ref, w_scr, b_scr, *, shift):
    tn = wa_ref.shape[1]

    @pl.when((pl.program_id(1) == 0) & (pl.program_id(2) == 0))
    def _():
        if shift:
            width = tn + LANES
            rows = 256
            for r in range(0, wa_ref.shape[0], rows):
                w = jnp.concatenate([wa_ref[r:r + rows, :], wn_ref[r:r + rows, :]], axis=1)
                w_scr[r:r + rows, :] = pltpu.roll(w, width - shift, 1)[:, :tn].astype(BF16)
            b = jnp.concatenate([ba_ref[...], bn_ref[...]], axis=1)
            b_scr[...] = pltpu.roll(b, width - shift, 1)[:, :tn]
        else:
            w_scr[...] = wa_ref[...].astype(BF16)
            b_scr[...] = ba_ref[...]

    o_ref[0] = (_dot(x_ref[0], w_scr[...]) + b_scr[...]).astype(o_ref.dtype)


def in_proj(x, w, bias, layer, col0, n_cols, out_dtype=F32):
    bsz, n, k = x.shape
    tm, tn = n // 2, 1024
    shift = col0 % LANES
    cb = (col0 - shift) // tn
    nb = (col0 - shift) // LANES + tn // LANES
    return pl.pallas_call(
        functools.partial(_in_proj_kernel, shift=shift), grid=(n_cols // tn, bsz, n // tm),
        in_specs=[pl.BlockSpec((1, tm, k), lambda j, b, t: (b, t, 0)),
                  pl.BlockSpec((None, k, tn), lambda j, b, t: (layer, 0, cb + j)),
                  pl.BlockSpec((None, k, LANES), lambda j, b, t: (layer, 0, nb + j * (tn // LANES))),
                  pl.BlockSpec((None, 1, tn), lambda j, b, t: (layer, 0, cb + j)),
                  pl.BlockSpec((None, 1, LANES), lambda j, b, t: (layer, 0, nb + j * (tn // LANES)))],
        out_specs=pl.BlockSpec((1, tm, tn), lambda j, b, t: (b, t, j)),
        out_shape=jax.ShapeDtypeStruct((bsz, n, n_cols), out_dtype),
        scratch_shapes=[pltpu.VMEM((k, tn), BF16), pltpu.VMEM((1, tn), F32)],
        compiler_params=_params(("arbitrary", "arbitrary", "arbitrary"), 56),
        name="in_proj",
    )(x, w, w, bias, bias)


def _mm_resid_kernel(x_ref, w_ref, z_ref, gctx_ref, glat_ref, o_ref, *, n_ctx):
    acc = _dot(x_ref[0].astype(BF16), w_ref[...].astype(BF16))
    tm = acc.shape[0]
    row = pl.program_id(2) * tm + lax.broadcasted_iota(jnp.int32, (tm, 1), 0)
    gate = jnp.where(row < n_ctx, gctx_ref[...], glat_ref[...])
    o_ref[0] = z_ref[0] + gate * acc


def matmul_resid(x, w, z, mods, layer, gate_chunk, n_ctx):
    bsz, n, k = x.shape
    nc = w.shape[-1]
    tm, tn = 768, 512
    gate_spec = lambda ctx: pl.BlockSpec(
        (None, None, 1, tn),
        lambda j, b, t: (layer, bsz if ctx else b, 0, gate_chunk * (nc // tn) + j))
    return pl.pallas_call(
        functools.partial(_mm_resid_kernel, n_ctx=n_ctx), grid=(nc // tn, bsz, n // tm),
        in_specs=[pl.BlockSpec((1, tm, k), lambda j, b, t: (b, t, 0)),
                  pl.BlockSpec((None, k, tn), lambda j, b, t: (layer, 0, j)),
                  pl.BlockSpec((1, tm, tn), lambda j, b, t: (b, t, j)),
                  gate_spec(True), gate_spec(False)],
        out_specs=pl.BlockSpec((1, tm, tn), lambda j, b, t: (b, t, j)),
        out_shape=jax.ShapeDtypeStruct((bsz, n, nc), F32),
        compiler_params=_params(("arbitrary", "arbitrary", "arbitrary"), 48),
        name="matmul_resid",
    )(x, w, z, mods, mods)


def _conv_kernel(x_ref, w_ref, b_ref, o_ref, ot_ref, *, n_ctx):
    x = x_ref[0]
    n = x.shape[0]
    row = lax.broadcasted_iota(jnp.int32, x.shape, 0)
    prev = jnp.where((row == 0) | (row == n_ctx), 0.0, pltpu.roll(x, 1, 0))
    nxt = jnp.where((row == n_ctx - 1) | (row == n - 1), 0.0, pltpu.roll(x, n - 1, 0))
    w = w_ref[...]
    y = b_ref[...] + prev * w[0:1] + x * w[1:2] + nxt * w[2:3]
    y = y * jax.nn.sigmoid(y)
    k_tiles = pl.num_programs(1) // 2
    y = y * jnp.where(pl.program_id(1) >= k_tiles, ML_DH ** -0.5, 1.0)
    o_ref[0] = y.astype(o_ref.dtype)
    ot_ref[0] = y.T.astype(ot_ref.dtype)


def conv_silu(proj, conv_w, conv_b, layer, n_ctx):
    bsz, n, _ = proj.shape
    tc = 256
    c2 = 2 * MIX_W
    return pl.pallas_call(
        functools.partial(_conv_kernel, n_ctx=n_ctx),
        grid=(bsz, c2 // tc),
        in_specs=[pl.BlockSpec((1, n, tc), lambda b, c: (b, 0, c)),
                  pl.BlockSpec((None, conv_w.shape[1], tc), lambda b, c: (layer, 0, c)),
                  pl.BlockSpec((None, 1, tc), lambda b, c: (layer, 0, c))],
        out_specs=[pl.BlockSpec((1, n, tc), lambda b, c: (b, 0, c)),
                   pl.BlockSpec((1, tc, n), lambda b, c: (b, c, 0))],
        out_shape=[jax.ShapeDtypeStruct((bsz, n, c2), BF16),
                   jax.ShapeDtypeStruct((bsz, c2, n), BF16)],
        compiler_params=_params(("arbitrary", "arbitrary"), 48),
        name="conv_silu",
    )(proj, conv_w, conv_b.reshape(conv_b.shape[0], 1, c2))


def _log_sigmoid(x):
    return jnp.minimum(x, 0.0) - jnp.log(1.0 + jnp.exp(-jnp.abs(x)))


def _mlstm_kernel(qf_ref, kf_ref, ktf_ref, vf_ref, gf_ref, gtf_ref,
                  qb_ref, kb_ref, ktb_ref, vb_ref, gb_ref, gtb_ref,
                  hf_ref, hb_ref, c_ref, m_ref):
    t = qf_ref.shape[1]
    dh = ML_DH

    @pl.when(pl.program_id(1) == 0)
    def _():
        c_ref[...] = jnp.zeros_like(c_ref)
        m_ref[...] = jnp.zeros_like(m_ref)

    r_i = lax.broadcasted_iota(jnp.int32, (t, t), 0)
    c_i = lax.broadcasted_iota(jnp.int32, (t, t), 1)
    low = c_i <= r_i
    upp = c_i >= r_i
    low_b = low.astype(BF16)
    upp_b = upp.astype(BF16)
    ones_col = (lax.broadcasted_iota(jnp.int32, (t, LANES), 1) == 0).astype(F32)

    dirs = ((qf_ref, kf_ref, ktf_ref, vf_ref, gf_ref, gtf_ref, hf_ref, low, low_b, upp_b),
            (qb_ref, kb_ref, ktb_ref, vb_ref, gb_ref, gtb_ref, hb_ref, upp, upp_b, low_b))
    for d, (q_ref, k_ref, kt_ref, v_ref, g_ref, gt_ref, h_ref, mask, col_tri, row_tri) in enumerate(dirs):
        g = g_ref[0]
        gt = gt_ref[0]
        lf = _log_sigmoid(g)
        lft = _log_sigmoid(gt)
        b_cols = sum(_dot(col_tri, p) for p in _split3(lf))
        b_rows = sum(_dot(p, row_tri) for p in _split3(lft))
        last = t - 1 if d == 0 else 0
        for hd in range(ML_HEADS):
            ci = 2 * d * ML_HEADS + hd
            cf = ci + ML_HEADS
            idx = d * ML_HEADS + hd
            sl = slice(hd * dh, (hd + 1) * dh)
            q = q_ref[0, :, sl]
            k = k_ref[0, :, sl]
            kt = kt_ref[0, sl, :]
            v = v_ref[0, :, sl]
            i_col = g[:, ci:ci + 1]
            i_row = gt[ci:ci + 1, :]
            b_col = b_cols[:, cf:cf + 1]
            b_row = b_rows[cf:cf + 1, :]
            b_tot = b_col[last:last + 1, :]
            m_prev = m_ref[idx][:, :1]
            c_prev = c_ref[idx]

            log_inter = b_col + m_prev
            log_intra = jnp.where(mask, b_col - b_row + i_row, -jnp.inf)
            m_t = jnp.maximum(log_inter, jnp.max(log_intra, axis=-1, keepdims=True))
            w_inter = jnp.exp(log_inter - m_t)
            s = _dot_nt(q, k) * jnp.exp(log_intra - m_t)
            v_ext = jnp.concatenate([v, ones_col], axis=1)
            nd = w_inter * _dot(q, c_prev.astype(BF16)) + _dot(s.astype(BF16), v_ext.astype(BF16))
            den = nd[:, dh:dh + 1]
            h_ref[0, :, sl] = nd[:, :dh] * (1.0 / jnp.maximum(jnp.abs(den), jnp.exp(-m_t)))

            log_w = b_tot - b_col + i_col
            m_new = jnp.maximum(b_tot + m_prev, jnp.max(log_w, axis=0, keepdims=True))
            decay = jnp.exp(b_tot + m_prev - m_new)
            w_s = jnp.exp(log_w - m_new)
            c_ref[idx] = decay * c_prev + _dot(kt, (v_ext * w_s).astype(BF16))
            m_ref[idx] = jnp.broadcast_to(m_new, (1, LANES))


def mlstm(qk, qkt, proj, gates, gates_t, n_ctx):
    bsz, n, _ = qk.shape
    t = SEQ_CHUNK
    nc = n // t
    c0 = n_ctx // t
    w = MIX_W
    fwd = lambda j: j
    bwd = lambda j: _rev_chunk(j, c0, nc)

    def specs(ch):
        return [pl.BlockSpec((1, t, w), lambda b, j: (b, ch(j), 0)),
                pl.BlockSpec((1, t, w), lambda b, j: (b, ch(j), 1)),
                pl.BlockSpec((1, w, t), lambda b, j: (b, 1, ch(j))),
                pl.BlockSpec((1, t, w), lambda b, j: (b, ch(j), COL_VA // w)),
                pl.BlockSpec((1, t, LANES), lambda b, j: (b, ch(j), 0)),
                pl.BlockSpec((1, N_GATES, t), lambda b, j: (b, 0, ch(j)))]

    args = [qk, qk, qkt, proj, gates, gates_t]
    out = jax.ShapeDtypeStruct((bsz, n, w), F32)
    return pl.pallas_call(
        _mlstm_kernel, grid=(bsz, nc),
        in_specs=specs(fwd) + specs(bwd),
        out_specs=[pl.BlockSpec((1, t, w), lambda b, j: (b, fwd(j), 0)),
                   pl.BlockSpec((1, t, w), lambda b, j: (b, bwd(j), 0))],
        out_shape=[out, out],
        scratch_shapes=[pltpu.VMEM((2 * ML_HEADS, ML_DH, ML_DH + LANES), F32),
                        pltpu.VMEM((2 * ML_HEADS, 1, LANES), F32)],
        compiler_params=_params(("arbitrary", "arbitrary"), 48),
        name="mlstm",
    )(*(args + args))


def _ya_kernel(hf_ref, hb_ref, oa_ref, g_ref, o_ref):
    h = hf_ref[0] + hb_ref[0]
    parts = []
    for hd in range(ML_HEADS):
        x = h[:, hd * ML_DH:(hd + 1) * ML_DH]
        parts.append(x * lax.rsqrt(jnp.mean(x * x, axis=-1, keepdims=True) + EPS))
    y = jnp.concatenate(parts, axis=1) * g_ref[...]
    o_ref[0] = (jax.nn.sigmoid(oa_ref[0]) * y).astype(o_ref.dtype)


def mlstm_out(hf, hb, proj, norm_g, layer):
    bsz, n, w = hf.shape
    tm = SEQ_CHUNK
    tile = pl.BlockSpec((1, tm, w), lambda b, t: (b, t, 0))
    return pl.pallas_call(
        _ya_kernel, grid=(bsz, n // tm),
        in_specs=[tile, tile, pl.BlockSpec((1, tm, w), lambda b, t: (b, t, COL_OA // w)),
                  pl.BlockSpec((None, 1, w), lambda b, t: (layer, 0, 0))],
        out_specs=tile, out_shape=jax.ShapeDtypeStruct((bsz, n, w), BF16),
        compiler_params=_params(("arbitrary", "arbitrary"), 40),
        name="mlstm_out",
    )(hf, hb, proj, norm_g.reshape(norm_g.shape[0], 1, w))


def _s5_kernel(uf_ref, ub_ref, bm_ref, cm_ref, a_ref, yf_ref, yb_ref, lhs_ref, re_ref, im_ref, y_ref, carry_ref):
    nb, t, _ = uf_ref.shape
    rows = t * nb
    sw = S5_BUNDLE * S5_STATE
    tiles_per_bundle = sw // LANES
    in_slabs = MIX_W // LANES

    @pl.when(pl.program_id(0) == 0)
    def _():
        carry_ref[...] = jnp.zeros_like(carry_ref)

    for d, u_ref in enumerate((uf_ref, ub_ref)):
        for b in range(nb):
            for s in range(in_slabs):
                lhs_ref[d, s, pl.ds(b, t, stride=nb), :] = u_ref[b, :, s * LANES:(s + 1) * LANES]
        for kb in range(S5_NB):
            lhs = jnp.concatenate([lhs_ref[d, 2 * kb], lhs_ref[d, 2 * kb + 1]], axis=1).astype(BF16)
            bu = _dot(lhs, bm_ref[d, kb])
            half = kb // (S5_NB // S5_HALVES)
            for lt in range(tiles_per_bundle):
                s = (kb % (S5_NB // S5_HALVES)) * tiles_per_bundle + lt
                re_ref[d, s, pl.ds(half, rows, stride=S5_HALVES), :] = bu[:, lt * LANES:(lt + 1) * LANES]
                im_ref[d, s, pl.ds(half, rows, stride=S5_HALVES), :] = bu[:, sw + lt * LANES:sw + (lt + 1) * LANES]

    for g in range(S5_SLABS // S5_GRP):
        slabs = tuple(range(g * S5_GRP, (g + 1) * S5_GRP))
        a_vals = [[(a_ref[d, 0, s], a_ref[d, 1, s]) for s in slabs] for d in range(2)]
        init = tuple(carry_ref[d, c, s] for d in range(2) for s in slabs for c in range(2))

        def body(i, xs, slabs=slabs, a_vals=a_vals):
            out = []
            for d in range(2):
                tok = i if d == 0 else t - 1 - i
                r0 = pl.multiple_of(tok * SUBLANES, SUBLANES)
                for j, s in enumerate(slabs):
                    xr, xi = xs[2 * (d * S5_GRP + j)], xs[2 * (d * S5_GRP + j) + 1]
                    ar, ai = a_vals[d][j]
                    nr = ar * xr - ai * xi + re_ref[d, s, pl.ds(r0, SUBLANES), :]
                    ni = ar * xi + ai * xr + im_ref[d, s, pl.ds(r0, SUBLANES), :]
                    re_ref[d, s, pl.ds(r0, SUBLANES), :] = nr
                    im_ref[d, s, pl.ds(r0, SUBLANES), :] = ni
                    out += [nr, ni]
            return tuple(out)

        fin = lax.fori_loop(0, t, body, init, unroll=2)
        for d in range(2):
            for j, s in enumerate(slabs):
                for c in range(2):
                    carry_ref[d, c, s] = fin[2 * (d * S5_GRP + j) + c]

    for d, out_ref in enumerate((yf_ref, yb_ref)):
        for kb in range(S5_NB):
            half = kb // (S5_NB // S5_HALVES)
            s0 = (kb % (S5_NB // S5_HALVES)) * tiles_per_bundle
            sr = jnp.concatenate([re_ref[d, s0 + lt, pl.ds(half, rows, stride=S5_HALVES), :]
                                  for lt in range(tiles_per_bundle)], axis=1).astype(BF16)
            si = jnp.concatenate([im_ref[d, s0 + lt, pl.ds(half, rows, stride=S5_HALVES), :]
                                  for lt in range(tiles_per_bundle)], axis=1).astype(BF16)
            y = _dot(sr, cm_ref[kb, 0]) + _dot(si, cm_ref[kb, 1])
            y_ref[d, 2 * kb] = y[:, :LANES]
            y_ref[d, 2 * kb + 1] = y[:, LANES:]
        for b in range(nb):
            for s in range(in_slabs):
                out_ref[b, :, s * LANES:(s + 1) * LANES] = y_ref[d, s, pl.ds(b, t, stride=nb), :]


def s5_scan(proj, bmat, cmat, atab, n_ctx):
    bsz, n, _ = proj.shape
    assert bsz * S5_HALVES == SUBLANES, "the scan layout puts (sample, state half) on the 8 sublanes"
    t = S5_TC
    nc = n // t
    c0 = n_ctx // t
    w = MIX_W
    rows = t * bsz
    bwd = lambda j: _rev_chunk(j, c0, nc)
    out = jax.ShapeDtypeStruct((bsz, n, w), F32)
    const = lambda a: pl.BlockSpec(a.shape, lambda j: (0,) * a.ndim)
    return pl.pallas_call(
        _s5_kernel, grid=(nc,),
        in_specs=[pl.BlockSpec((bsz, t, w), lambda j: (0, j, COL_US // w)),
                  pl.BlockSpec((bsz, t, w), lambda j: (0, bwd(j), COL_US // w)),
                  const(bmat), const(cmat), const(atab)],
        out_specs=[pl.BlockSpec((bsz, t, w), lambda j: (0, j, 0)),
                   pl.BlockSpec((bsz, t, w), lambda j: (0, bwd(j), 0))],
        out_shape=[out, out],
        scratch_shapes=[pltpu.VMEM((2, w // LANES, rows, LANES), F32),
                        pltpu.VMEM((2, S5_SLABS, rows * S5_HALVES, LANES), F32),
                        pltpu.VMEM((2, S5_SLABS, rows * S5_HALVES, LANES), F32),
                        pltpu.VMEM((2, w // LANES, rows, LANES), F32),
                        pltpu.VMEM((2, 2, S5_SLABS, SUBLANES, LANES), F32)],
        compiler_params=_params(("arbitrary",), 56),
        name="s5_scan",
    )(proj, proj, bmat, cmat, atab)


def s5_operands(lam_re, lam_im, log_dt, b_re, b_im, c_re, c_im):
    lam_re, lam_im = lam_re.astype(F32), lam_im.astype(F32)
    dt = jnp.exp(log_dt.astype(F32))[..., None]
    mag = jnp.exp(lam_re * dt)
    a_re, a_im = mag * jnp.cos(lam_im * dt), mag * jnp.sin(lam_im * dt)
    nr, ni = a_re - 1.0, a_im
    den = lam_re * lam_re + lam_im * lam_im
    fr = (nr * lam_re + ni * lam_im) / den
    fi = (ni * lam_re - nr * lam_im) / den
    bb_re = fr[..., None] * b_re - fi[..., None] * b_im
    bb_im = fr[..., None] * b_im + fi[..., None] * b_re
    eye = jnp.eye(S5_BUNDLE, dtype=F32)

    def blockdiag_b(bb):
        x = bb.reshape(2, S5_NB, S5_BUNDLE, S5_STATE, S5_GROUP)
        return jnp.einsum('dkgpi,gh->dkgihp', x, eye).reshape(
            2, S5_NB, S5_BUNDLE * S5_GROUP, S5_BUNDLE * S5_STATE)

    bmat = jnp.concatenate([blockdiag_b(bb_re), blockdiag_b(bb_im)], -1).astype(BF16)

    def blockdiag_c(c):
        x = c.astype(F32).reshape(S5_NB, S5_BUNDLE, S5_GROUP, S5_STATE)
        return jnp.einsum('kgip,gh->kgphi', x, eye).reshape(
            S5_NB, S5_BUNDLE * S5_STATE, S5_BUNDLE * S5_GROUP)

    cmat = jnp.stack([blockdiag_c(c_re), -blockdiag_c(c_im)], 1).astype(BF16)

    def slab_table(a):
        a4 = a.reshape(2, S5_HALVES, S5_SLABS, LANES)
        return jnp.take(a4, jnp.arange(SUBLANES) % S5_HALVES, axis=1).transpose(0, 2, 1, 3)

    atab = jnp.stack([slab_table(a_re), slab_table(a_im)], 1)
    return bmat, cmat, atab


def _glu_kernel(yf_ref, yb_ref, u_ref, d_ref, w_ref, b_ref, o_ref):
    y = d_ref[...] * u_ref[0] + yf_ref[0] + yb_ref[0]
    z = jax.nn.gelu(y)
    o_ref[0] = (z * jax.nn.sigmoid(_dot(z.astype(BF16), w_ref[...].astype(BF16)) + b_ref[...])).astype(o_ref.dtype)


def s5_glu(yf, yb, proj, d_skip, glu_w, glu_b, layer):
    bsz, n, w = yf.shape
    tm = SEQ_CHUNK
    tile = pl.BlockSpec((1, tm, w), lambda b, t: (b, t, 0))
    vec = pl.BlockSpec((None, 1, w), lambda b, t: (layer, 0, 0))
    return pl.pallas_call(
        _glu_kernel, grid=(bsz, n // tm),
        in_specs=[tile, tile, pl.BlockSpec((1, tm, w), lambda b, t: (b, t, COL_US // w)),
                  vec, pl.BlockSpec((None, w, w), lambda b, t: (layer, 0, 0)), vec],
        out_specs=tile, out_shape=jax.ShapeDtypeStruct((bsz, n, w), BF16),
        compiler_params=_params(("arbitrary", "arbitrary"), 40),
        name="s5_glu",
    )(yf, yb, proj, d_skip.reshape(-1, 1, w), glu_w, glu_b.reshape(-1, 1, w))


def _da_prep_kernel(x_ref, g_ref, cos_ref, sin_ref, o_ref):
    cos = cos_ref[...]
    sin = sin_ref[...]
    lane = lax.broadcasted_iota(jnp.int32, cos.shape, 1)
    first = lane < DA_DQK
    quarter = DA_DQK // 4
    up = (lane & quarter) == 0
    scale = jnp.where(pl.program_id(2) == 0, DA_DQK ** -0.5 * math.log2(math.e), 1.0)
    for hd in range(DA_HEADS):
        sl = slice(hd * LANES, (hd + 1) * LANES)
        x = x_ref[0, :, sl]
        x2 = x * x
        ms0 = jnp.sum(jnp.where(first, x2, 0.0), axis=-1, keepdims=True)
        ms1 = jnp.sum(jnp.where(first, 0.0, x2), axis=-1, keepdims=True)
        ms = jnp.where(first, ms0, ms1) * (1.0 / DA_DQK)
        y = x * lax.rsqrt(ms + EPS) * g_ref[:, sl]
        partner = jnp.where(up, pltpu.roll(y, LANES - quarter, 1), pltpu.roll(y, quarter, 1))
        o_ref[0, :, sl] = ((y * cos + partner * sin) * scale).astype(o_ref.dtype)


def da_prep(proj, qn_g, kn_g, cos_t, sin_t, layer):
    bsz, n, _ = proj.shape
    tm = SEQ_CHUNK
    w = DA_HEADS * LANES
    reps = w // DA_DQK
    gains = jnp.concatenate([jnp.tile(qn_g, (1, reps)), jnp.tile(kn_g, (1, reps))], -1)
    gains = gains.reshape(gains.shape[0], 1, 2 * w)
    return pl.pallas_call(
        _da_prep_kernel, grid=(bsz, n // tm, 2),
        in_specs=[pl.BlockSpec((1, tm, w), lambda b, t, j: (b, t, COL_QD // w + j)),
                  pl.BlockSpec((None, 1, w), lambda b, t, j: (layer, 0, j)),
                  pl.BlockSpec((tm, LANES), lambda b, t, j: (t, 0)),
                  pl.BlockSpec((tm, LANES), lambda b, t, j: (t, 0))],
        out_specs=pl.BlockSpec((1, tm, w), lambda b, t, j: (b, t, j)),
        out_shape=jax.ShapeDtypeStruct((bsz, n, 2 * w), BF16),
        compiler_params=_params(("arbitrary", "arbitrary", "arbitrary"), 40),
        name="da_prep",
    )(proj, gains, cos_t, sin_t)


def rope_tables(n, n_ctx):
    lat = jnp.arange(n) - n_ctx
    rows = (lat // GRID_W).astype(F32)
    cols = (lat % GRID_W).astype(F32)
    lane = jnp.arange(LANES)
    in_comp = lane % DA_DQK
    quarter = DA_DQK // 4
    freqs = ROPE_BASE ** (-(in_comp % quarter).astype(F32) / quarter)
    pos = jnp.where((in_comp < DA_DQK // 2)[None, :], rows[:, None], cols[:, None])
    ang = pos * freqs[None, :]
    sign = jnp.where((in_comp % (2 * quarter)) < quarter, -1.0, 1.0)
    is_lat = (lat >= 0)[:, None]
    cos_t = jnp.where(is_lat, jnp.cos(ang), 1.0)
    sin_t = jnp.where(is_lat, jnp.sin(ang) * sign[None, :], 0.0)
    return cos_t.astype(F32), sin_t.astype(F32)


DA_HEADS_PER_STEP = 2


def _diff_attn_kernel(scal_ref, q_ref, k_ref, v_ref, g_ref, o_ref, *, n_ctx, ctx_tiles):
    lam = scal_ref[0]
    out_scale = scal_ref[1]
    lane = lax.broadcasted_iota(jnp.int32, (q_ref.shape[1], LANES), 1)
    first = lane < DA_DQK

    def exp_rows(s):
        e = jnp.exp2(s - jnp.max(s, axis=-1, keepdims=True))
        return e, jnp.sum(e, axis=-1, keepdims=True)

    def attend(nk):
        for hh in range(DA_HEADS_PER_STEP):
            sl = slice(hh * LANES, (hh + 1) * LANES)
            q = q_ref[0, :, sl]
            zero = jnp.zeros_like(q)
            k = k_ref[0, :nk, sl]
            e0, l0 = exp_rows(_dot_nt(jnp.where(first, q, zero), k))
            e1, l1 = exp_rows(_dot_nt(jnp.where(first, zero, q), k))
            a = e0 - (lam * l0 * (1.0 / l1)) * e1
            o = _dot(a.astype(BF16), v_ref[0, :nk, sl].astype(BF16)) * (1.0 / l0)
            o = o * lax.rsqrt(jnp.mean(o * o, axis=-1, keepdims=True) + EPS) * g_ref[...]
            o_ref[0, :, sl] = (o * out_scale).astype(o_ref.dtype)

    @pl.when(pl.program_id(2) < ctx_tiles)
    def _():
        attend(n_ctx)

    @pl.when(pl.program_id(2) >= ctx_tiles)
    def _():
        attend(k_ref.shape[1])


def diff_attn(qk_hat, proj, scal, sub_g, layer, n_ctx):
    bsz, n, _ = qk_hat.shape
    tq = SEQ_CHUNK
    w = DA_HEADS_PER_STEP * LANES
    return pl.pallas_call(
        functools.partial(_diff_attn_kernel, n_ctx=n_ctx, ctx_tiles=n_ctx // tq),
        grid=(bsz, DA_HEADS // DA_HEADS_PER_STEP, n // tq),
        in_specs=[pl.BlockSpec(memory_space=pltpu.SMEM),
                  pl.BlockSpec((1, tq, w), lambda b, h, t: (b, t, h)),
                  pl.BlockSpec((1, n, w), lambda b, h, t: (b, 0, DA_HEADS // DA_HEADS_PER_STEP + h)),
                  pl.BlockSpec((1, n, w), lambda b, h, t: (b, 0, COL_VD // w + h)),
                  pl.BlockSpec((None, 1, LANES), lambda b, h, t: (layer, 0, 0))],
        out_specs=pl.BlockSpec((1, tq, w), lambda b, h, t: (b, t, h)),
        out_shape=jax.ShapeDtypeStruct((bsz, n, DA_HEADS * DA_DV), BF16),
        compiler_params=_params(("arbitrary", "arbitrary", "arbitrary"), 48),
        name="diff_attn",
    )(scal, qk_hat, qk_hat, proj, sub_g.reshape(sub_g.shape[0], 1, DA_DV))


def _merge_kernel(ya_ref, yb_ref, yc_ref, p_ref, ga_ref, gb_ref, gc_ref, o_ref):
    acc = _dot(ya_ref[0], p_ref[0].astype(BF16)) * jax.nn.sigmoid(ga_ref[0])
    acc += _dot(yb_ref[0], p_ref[1].astype(BF16)) * jax.nn.sigmoid(gb_ref[0])
    acc += _dot(yc_ref[0], p_ref[2].astype(BF16)) * jax.nn.sigmoid(gc_ref[0])
    o_ref[0] = acc.astype(o_ref.dtype)


def branch_merge(ya, yb, yc, proj, branch_proj, layer):
    bsz, n, w = ya.shape
    d = branch_proj.shape[-1]
    tm, tn = 768, 512
    y_tile = pl.BlockSpec((1, tm, w), lambda j, b, t: (b, t, 0))
    gate = lambda r: pl.BlockSpec((1, tm, tn), lambda j, b, t: (b, t, (COL_GB + r * d) // tn + j))
    return pl.pallas_call(
        _merge_kernel, grid=(d // tn, bsz, n // tm),
        in_specs=[y_tile, y_tile, y_tile,
                  pl.BlockSpec((None, N_BRANCH, w, tn), lambda j, b, t: (layer, 0, 0, j)),
                  gate(0), gate(1), gate(2)],
        out_specs=pl.BlockSpec((1, tm, tn), lambda j, b, t: (b, t, j)),
        out_shape=jax.ShapeDtypeStruct((bsz, n, d), BF16),
        compiler_params=_params(("arbitrary", "arbitrary", "arbitrary"), 48),
        name="branch_merge",
    )(ya, yb, yc, branch_proj, proj, proj, proj)


INFO_E, INFO_POS, INFO_W = 0, 2, 4


def _router_kernel(h_ref, w_ref, b_ref, info_ref, cnt_ref):
    tm = h_ref.shape[1]

    @pl.when((pl.program_id(0) == 0) & (pl.program_id(1) == 0))
    def _():
        cnt_ref[...] = jnp.zeros_like(cnt_ref)

    h_parts = _split3(h_ref[0])
    w_parts = _split3(w_ref[...])
    logits = (_dot_nt(w_parts[0], h_parts[0]) + _dot_nt(w_parts[0], h_parts[1])
              + _dot_nt(w_parts[1], h_parts[0]))
    score = jax.nn.sigmoid(logits)
    sel = score + b_ref[...]
    rows = [sel[e:e + 1, :] for e in range(N_EXPERTS)]
    gscores = []
    for g in range(N_EXPERT_GROUPS):
        a, b, c, d = rows[g * EXPERTS_PER_GROUP:(g + 1) * EXPERTS_PER_GROUP]
        hi1, lo1 = jnp.maximum(a, b), jnp.minimum(a, b)
        hi2, lo2 = jnp.maximum(c, d), jnp.minimum(c, d)
        gscores.append(jnp.maximum(hi1, hi2) + jnp.maximum(jnp.minimum(hi1, hi2), jnp.maximum(lo1, lo2)))
    gmax = functools.reduce(jnp.maximum, gscores)
    taken = jnp.zeros_like(gmax, dtype=jnp.bool_)
    picked = []
    for g in range(N_EXPERT_GROUPS):
        is_g = (gscores[g] == gmax) & jnp.logical_not(taken)
        picked.append(is_g)
        taken = taken | is_g
    chosen = []
    for e in range(N_EXPERTS):
        g = e // EXPERTS_PER_GROUP
        rank = jnp.zeros_like(gmax)
        for o in range(g * EXPERTS_PER_GROUP, (g + 1) * EXPERTS_PER_GROUP):
            if o == e:
                continue
            ahead = (rows[o] > rows[e]) | ((rows[o] == rows[e]) & (o < e))
            rank = rank + ahead.astype(F32)
        chosen.append(picked[g] & (rank < float(TOP_K)))
    onehot = jnp.concatenate([c.astype(F32) for c in chosen], axis=0)
    total = jnp.sum(onehot * score, axis=0, keepdims=True)
    gates = onehot * score * (1.0 / total)

    r_i = lax.broadcasted_iota(jnp.int32, (tm, tm), 0)
    c_i = lax.broadcasted_iota(jnp.int32, (tm, tm), 1)
    incl = _dot(onehot.astype(BF16), (r_i <= c_i).astype(BF16))
    pos = incl - onehot + cnt_ref[:, :1]
    cnt_ref[...] = cnt_ref[...] + incl[:, tm - 1:tm]

    seen = jnp.zeros_like(gmax)
    rec = [jnp.zeros_like(gmax) for _ in range(3 * TOP_K)]
    for e in range(N_EXPERTS):
        oh = onehot[e:e + 1, :]
        for k, is_k in enumerate((oh * (1.0 - seen), oh * seen)):
            rec[INFO_E + k] += is_k * float(e)
            rec[INFO_POS + k] += is_k * pos[e:e + 1, :]
            rec[INFO_W + k] += is_k * gates[e:e + 1, :]
        seen = seen + oh
    pad = jnp.zeros((LANES - len(rec), tm), F32)
    info_ref[0] = jnp.concatenate(rec + [pad], axis=0).T


def moe_router(h2, router_w, router_b):
    bsz, n, d = h2.shape
    tm = SEQ_CHUNK
    return pl.pallas_call(
        _router_kernel, grid=(bsz, n // tm),
        in_specs=[pl.BlockSpec((1, tm, d), lambda b, t: (b, t, 0)),
                  pl.BlockSpec((N_EXPERTS, d), lambda b, t: (0, 0)),
                  pl.BlockSpec((N_EXPERTS, 1), lambda b, t: (0, 0))],
        out_specs=[pl.BlockSpec((1, tm, LANES), lambda b, t: (b, t, 0)),
                   pl.BlockSpec((N_EXPERTS, LANES), lambda b, t: (0, 0))],
        out_shape=[jax.ShapeDtypeStruct((bsz, n, LANES), F32),
                   jax.ShapeDtypeStruct((N_EXPERTS, LANES), F32)],
        compiler_params=_params(("arbitrary", "arbitrary"), 40),
        name="moe_router",
    )(h2, router_w.T, router_b.reshape(N_EXPERTS, 1))


def moe_plan(info, counts, n_tiles):
    m = info.shape[0] * info.shape[1]
    rec = info.reshape(m, LANES)
    cnt = counts[:, 0].astype(jnp.int32)
    padded = (cnt + MOE_TILE - 1) // MOE_TILE * MOE_TILE
    ends = jnp.cumsum(padded)
    starts = ends - padded
    experts = rec[:, INFO_E:INFO_E + TOP_K].astype(jnp.int32)
    dest = starts[experts] + rec[:, INFO_POS:INFO_POS + TOP_K].astype(jnp.int32)
    tile_start = jnp.arange(n_tiles, dtype=jnp.int32) * MOE_TILE
    tile_expert = jnp.minimum(jnp.sum(ends[None, :] <= tile_start[:, None], axis=1), N_EXPERTS - 1).astype(jnp.int32)
    n_used = (ends[-1] // MOE_TILE).astype(jnp.int32).reshape(1)
    return dest.reshape(-1), tile_expert, n_used


def _row_copy(src_ref, src_row, dst_ref, dst_row, sem):
    return pltpu.make_async_copy(src_ref.at[pl.ds(src_row, 1), :], dst_ref.at[pl.ds(dst_row, 1), :], sem)


def _dispatch_kernel(dest_ref, h_ref, xs_in_ref, xs_ref, sem):
    del xs_in_ref
    tm = h_ref.shape[0]
    base = pl.program_id(0) * tm

    def start(r, carry):
        for k in range(TOP_K):
            _row_copy(h_ref, r, xs_ref, dest_ref[(base + r) * TOP_K + k], sem).start()
        return carry

    def wait(r, carry):
        for k in range(TOP_K):
            _row_copy(h_ref, 0, xs_ref, 0, sem).wait()
        return carry

    lax.fori_loop(0, tm, start, 0, unroll=DMA_UNROLL)
    lax.fori_loop(0, tm, wait, 0, unroll=DMA_UNROLL)


def moe_dispatch(h2, dest, xs_init):
    m, d = h2.shape
    n_rows = xs_init.shape[0]
    tm = SEQ_CHUNK
    grid_spec = pltpu.PrefetchScalarGridSpec(
        num_scalar_prefetch=1, grid=(m // tm,),
        in_specs=[pl.BlockSpec((tm, d), lambda i, dest: (i, 0)),
                  pl.BlockSpec(memory_space=pl.ANY)],
        out_specs=pl.BlockSpec(memory_space=pl.ANY),
        scratch_shapes=[pltpu.SemaphoreType.DMA])
    return pl.pallas_call(
        _dispatch_kernel, grid_spec=grid_spec,
        out_shape=jax.ShapeDtypeStruct((n_rows, d), h2.dtype),
        input_output_aliases={2: 0},
        compiler_params=_params(("arbitrary",), 40),
        name="moe_dispatch",
    )(dest, h2, xs_init)


def _expert_kernel(te_ref, nu_ref, x_ref, wg_ref, wu_ref, wd_ref, o_ref):
    del te_ref

    @pl.when(pl.program_id(0) < nu_ref[0])
    def _():
        x = x_ref[...].astype(BF16)
        a = _dot(x, wg_ref[...].astype(BF16))
        act = (a * jax.nn.sigmoid(a)) * _dot(x, wu_ref[...].astype(BF16))
        o_ref[...] = _dot(act.astype(BF16), wd_ref[...].astype(BF16))

    @pl.when(pl.program_id(0) >= nu_ref[0])
    def _():
        o_ref[...] = jnp.zeros_like(o_ref)


def moe_experts(xs, tile_expert, n_used, w_gate, w_up, w_down, layer):
    n_rows, d = xs.shape
    f = w_gate.shape[-1]
    w_spec = lambda r, c: pl.BlockSpec((None, None, r, c), lambda i, te, nu: (layer, te[i], 0, 0))
    grid_spec = pltpu.PrefetchScalarGridSpec(
        num_scalar_prefetch=2, grid=(n_rows // MOE_TILE,),
        in_specs=[pl.BlockSpec((MOE_TILE, d), lambda i, te, nu: (i, 0)),
                  w_spec(d, f), w_spec(d, f), w_spec(f, d)],
        out_specs=pl.BlockSpec((MOE_TILE, d), lambda i, te, nu: (i, 0)))
    return pl.pallas_call(
        _expert_kernel, grid_spec=grid_spec,
        out_shape=jax.ShapeDtypeStruct((n_rows, d), F32),
        compiler_params=_params(("arbitrary",), 56),
        name="moe_experts",
    )(tile_expert, n_used, xs, w_gate, w_up, w_down)


def _combine_kernel(dest_ref, ys_ref, info_ref, z_ref, gate_ref, o_ref, buf_ref, sem):
    tm = z_ref.shape[1]
    base = (pl.program_id(0) * pl.num_programs(1) + pl.program_id(1)) * tm

    def start(r, carry):
        for k in range(TOP_K):
            _row_copy(ys_ref, dest_ref[(base + r) * TOP_K + k], buf_ref.at[k], r, sem).start()
        return carry

    def wait(r, carry):
        for k in range(TOP_K):
            _row_copy(ys_ref, 0, buf_ref.at[k], 0, sem).wait()
        return carry

    lax.fori_loop(0, tm, start, 0, unroll=DMA_UNROLL)
    lax.fori_loop(0, tm, wait, 0, unroll=DMA_UNROLL)
    info = info_ref[0]
    mix = info[:, INFO_W:INFO_W + 1] * buf_ref[0] + info[:, INFO_W + 1:INFO_W + 2] * buf_ref[1]
    o_ref[0] = z_ref[0] + gate_ref[...] * mix


def moe_combine(ys, dest, info, z, mods, layer, gate_chunk, n_ctx):
    bsz, n, d = z.shape
    tm = SEQ_CHUNK
    ctx_tiles = n_ctx // tm
    tile = pl.BlockSpec((1, tm, d), lambda b, t, dest: (b, t, 0))
    grid_spec = pltpu.PrefetchScalarGridSpec(
        num_scalar_prefetch=1, grid=(bsz, n // tm),
        in_specs=[pl.BlockSpec(memory_space=pl.ANY),
                  pl.BlockSpec((1, tm, LANES), lambda b, t, dest: (b, t, 0)),
                  tile,
                  pl.BlockSpec((None, None, 1, d),
                               lambda b, t, dest: (layer, jnp.where(t < ctx_tiles, bsz, b), 0, gate_chunk))],
        out_specs=tile,
        scratch_shapes=[pltpu.VMEM((TOP_K, tm, d), F32), pltpu.SemaphoreType.DMA])
    return pl.pallas_call(
        _combine_kernel, grid_spec=grid_spec,
        out_shape=jax.ShapeDtypeStruct(z.shape, F32),
        compiler_params=_params(("arbitrary", "arbitrary"), 40),
        name="moe_combine",
    )(dest, ys, info, z, mods)


def kernel(x, c, ctx, c_ctx, norm1_g, norm2_g, ada_w, ada_b, w_in, b_in, ml_conv_w, ml_conv_b, ml_norm_g, s5_lam_re, s5_lam_im, s5_log_dt, s5_b_re, s5_b_im, s5_c_re, s5_c_im, s5_d, s5_glu_w, s5_glu_b, da_q_norm_g, da_k_norm_g, da_lam_q1, da_lam_k1, da_lam_q2, da_lam_k2, da_sub_norm_g, branch_proj, w_out, router_w, router_b, exp_w_gate, exp_w_up, exp_w_down):
    bsz, n_lat, d = x.shape
    n_ctx = ctx.shape[1]
    n = n_ctx + n_lat
    depth = w_in.shape[0]
    assert n_ctx % SEQ_CHUNK == 0 and n_lat % SEQ_CHUNK == 0 and d == D_MODEL

    rows = -(-(bsz + 1) // SUBLANES) * SUBLANES
    cvec = jnp.zeros((rows, d), F32).at[:bsz].set(c).at[bsz].set(c_ctx)
    mods = ada_mod(cvec, ada_w, ada_b).reshape(depth, rows, 1, 6 * d)

    b_in3 = b_in[:, None, :]
    m_tok = bsz * n
    moe_tiles = m_tok * TOP_K // MOE_TILE + N_EXPERTS
    xs = jnp.zeros((moe_tiles * MOE_TILE, d), F32)
    cos_t, sin_t = rope_tables(n, n_ctx)
    z = jnp.concatenate([ctx, x], 1)
    for i in range(depth):
        lam_init = 0.8 - 0.6 * math.exp(-0.3 * i)
        (h,) = norm_mod(z, norm1_g, mods, i, 0, 1, n_ctx, (BF16,))
        proj_ml = in_proj(h, w_in, b_in3, i, 0, GATE_OFF)
        proj = in_proj(h, w_in, b_in3, i, REST_OFF, REST_COLS)
        gates = matmul_bias(h, w_in, b_in3, i, n // 2, LANES, GATE_OFF, LANES)
        gates_t = jnp.swapaxes(gates[..., :N_GATES], 1, 2)

        qk, qkt = conv_silu(proj_ml, ml_conv_w, ml_conv_b, i, n_ctx)
        hf, hb = mlstm(qk, qkt, proj_ml, gates, gates_t, n_ctx)
        ya = mlstm_out(hf, hb, proj_ml, ml_norm_g, i)

        bmat, cmat, atab = s5_operands(s5_lam_re[i], s5_lam_im[i], s5_log_dt[i], s5_b_re[i], s5_b_im[i],
                                       s5_c_re[i], s5_c_im[i])
        yf, ybk = s5_scan(proj, bmat, cmat, atab, n_ctx)
        yb = s5_glu(yf, ybk, proj, s5_d, s5_glu_w, s5_glu_b, i)

        qk_hat = da_prep(proj, da_q_norm_g, da_k_norm_g, cos_t, sin_t, i)
        lam = (jnp.exp(jnp.sum(da_lam_q1[i].astype(F32) * da_lam_k1[i].astype(F32)))
               - jnp.exp(jnp.sum(da_lam_q2[i].astype(F32) * da_lam_k2[i].astype(F32))) + lam_init)
        scal = jnp.stack([lam, jnp.asarray(1.0 - lam_init, F32)]).astype(F32)
        yc = diff_attn(qk_hat, proj, scal, da_sub_norm_g, i, n_ctx)

        merged = branch_merge(ya, yb, yc, proj, branch_proj, i)
        z = matmul_resid(merged, w_out, z, mods, i, 2, n_ctx)

        (h2,) = norm_mod(z, norm2_g, mods, i, 3, 4, n_ctx, (F32,))
        info, counts = moe_router(h2, router_w, router_b)
        dest, tile_expert, n_used = moe_plan(info, counts, moe_tiles)
        xs = moe_dispatch(h2.reshape(m_tok, d), dest, xs)
        ys = moe_experts(xs, tile_expert, n_used, exp_w_gate, exp_w_up, exp_w_down, i)
        z = moe_combine(ys, dest, info, z, mods, i, 5, n_ctx)
    return z[:, n_ctx:]
```

```python
import functools
import math

import jax
import jax.numpy as jnp
from jax import lax
from jax.experimental import pallas as pl
from jax.experimental.pallas import tpu as pltpu

F32 = jnp.float32
BF16 = jnp.bfloat16

D_MODEL = 2048
GRID_W = 64
MIX_W = 1024
N_BRANCH = 3
ML_HEADS = 4
ML_DH = MIX_W // ML_HEADS
S5_GROUP = 16
S5_GROUPS = MIX_W // S5_GROUP
S5_STATE = 64
S5_BUNDLE = 16
S5_NB = S5_GROUPS // S5_BUNDLE
S5_NS = S5_GROUPS * S5_STATE
DA_HEADS = 8
DA_DQK = 64
DA_DV = 2 * DA_DQK
ROPE_BASE = 10000.0
N_EXPERTS = 16
N_EXPERT_GROUPS = 4
EXPERTS_PER_GROUP = N_EXPERTS // N_EXPERT_GROUPS
TOP_K = 2
D_FF_EXPERT = 512
EPS = 1e-6

COL_QA, COL_KA, COL_VA, COL_OA = 0, 1024, 2048, 3072
COL_US, COL_QD, COL_KD, COL_VD, COL_GB = 0, 1024, 2048, 3072, 4096
GATE_OFF = 4 * MIX_W
N_GATES = 4 * ML_HEADS
REST_OFF = GATE_OFF + N_GATES
REST_COLS = 4 * MIX_W + N_BRANCH * D_MODEL

SEQ_CHUNK = 256
LANES = 128
SUBLANES = 8
MIB = 1024 * 1024

S5_TC = 64
S5_HALVES = 2
S5_SLABS = S5_NS // S5_HALVES // LANES
S5_GRP = 4
MOE_TILE = 512
DMA_UNROLL = 8


def _params(sem, vmem_mib):
    return pltpu.CompilerParams(dimension_semantics=sem, vmem_limit_bytes=vmem_mib * MIB)


def _dot(a, b):
    return jnp.dot(a, b, preferred_element_type=F32)


def _dot_nt(a, b):
    return lax.dot_general(a, b, (((1,), (1,)), ((), ())), preferred_element_type=F32)


def _split3(x):
    hi = x.astype(BF16)
    r = x - hi.astype(F32)
    mid = r.astype(BF16)
    lo = (r - mid.astype(F32)).astype(BF16)
    return hi, mid, lo


def _rev_chunk(j, n_ctx_chunks, n_chunks):
    return jnp.where(j < n_ctx_chunks, n_ctx_chunks - 1 - j, n_chunks - 1 - (j - n_ctx_chunks))


def _ada_kernel(c_ref, w_ref, b_ref, o_ref):
    c = c_ref[...]
    s = (c * jax.nn.sigmoid(c)).astype(BF16)
    o_ref[0] = _dot(s, w_ref[0].astype(BF16)) + b_ref[0]


def ada_mod(cvec, ada_w, ada_b):
    depth, d, n6 = ada_w.shape
    rows = cvec.shape[0]
    tn = 1024
    return pl.pallas_call(
        _ada_kernel,
        grid=(depth, n6 // tn),
        in_specs=[pl.BlockSpec((rows, d), lambda l, j: (0, 0)),
                  pl.BlockSpec((1, d, tn), lambda l, j: (l, 0, j)),
                  pl.BlockSpec((1, 1, tn), lambda l, j: (l, 0, j))],
        out_specs=pl.BlockSpec((1, rows, tn), lambda l, j: (l, 0, j)),
        out_shape=jax.ShapeDtypeStruct((depth, rows, n6), F32),
        compiler_params=_params(("arbitrary", "arbitrary"), 40),
        name="ada_mod",
    )(cvec, ada_w, ada_b.reshape(depth, 1, n6))


def _mod_spec(layer, chunk, batch, ctx_tiles):
    return pl.BlockSpec((None, None, 1, D_MODEL),
                        lambda b, t: (layer, jnp.where(t < ctx_tiles, batch, b), 0, chunk))


def _norm_mod_kernel(z_ref, g_ref, shift_ref, scale_ref, *outs):
    z = z_ref[0]
    y = z * lax.rsqrt(jnp.mean(z * z, axis=-1, keepdims=True) + EPS) * g_ref[...]
    h = y * (1.0 + scale_ref[...]) + shift_ref[...]
    for o in outs:
        o[0] = h.astype(o.dtype)


def norm_mod(z, norm_g, mods, layer, shift_chunk, scale_chunk, n_ctx, out_dtypes):
    bsz, n, d = z.shape
    tm = SEQ_CHUNK
    ctx_tiles = n_ctx // tm
    tile = pl.BlockSpec((1, tm, d), lambda b, t: (b, t, 0))
    return pl.pallas_call(
        _norm_mod_kernel, grid=(bsz, n // tm),
        in_specs=[tile, pl.BlockSpec((None, 1, d), lambda b, t: (layer, 0, 0)),
                  _mod_spec(layer, shift_chunk, bsz, ctx_tiles), _mod_spec(layer, scale_chunk, bsz, ctx_tiles)],
        out_specs=[tile] * len(out_dtypes),
        out_shape=[jax.ShapeDtypeStruct(z.shape, dt) for dt in out_dtypes],
        compiler_params=_params(("arbitrary", "arbitrary"), 40),
        name="norm_mod",
    )(z, norm_g.reshape(norm_g.shape[0], 1, d), mods, mods)


def _in_proj_kernel(x_ref, wt_ref, b_ref, o_ref, w_scr):
    @pl.when((pl.program_id(1) == 0) & (pl.program_id(2) == 0))
    def _():
        rows = 256
        for r in range(0, wt_ref.shape[0], rows):
            w_scr[:, r:r + rows] = wt_ref[r:r + rows, :].T.astype(BF16)

    o_ref[0] = (_dot(x_ref[0], w_scr[...]) + b_ref[...]).astype(o_ref.dtype)


def in_proj(x, w_t, bias, layer, row0, out_dtype=F32):
    bsz, n, k = x.shape
    n_cols = bias.shape[-1]
    tm, tn = n // 2, 1024
    return pl.pallas_call(
        _in_proj_kernel, grid=(n_cols // tn, bsz, n // tm),
        in_specs=[pl.BlockSpec((1, tm, k), lambda j, b, t: (b, t, 0)),
                  pl.BlockSpec((None, pl.Element(tn), pl.Element(k)), lambda j, b, t: (layer, pl.multiple_of(row0 + j * tn, SUBLANES), 0)),
                  pl.BlockSpec((None, 1, tn), lambda j, b, t: (layer, 0, j))],
        out_specs=pl.BlockSpec((1, tm, tn), lambda j, b, t: (b, t, j)),
        out_shape=jax.ShapeDtypeStruct((bsz, n, n_cols), out_dtype),
        scratch_shapes=[pltpu.VMEM((k, tn), BF16)],
        compiler_params=_params(("arbitrary", "arbitrary", "arbitrary"), 56),
        name="in_proj",
    )(x, w_t, bias)


def _gates_kernel(x_ref, wt_ref, b_ref, o_ref):
    o_ref[0] = _dot_nt(x_ref[0], wt_ref[...].astype(BF16)) + b_ref[...]


def gate_proj(x, w_t, bias, layer, row0):
    bsz, n, k = x.shape
    tm = n // 2
    rb = row0 // LANES
    return pl.pallas_call(
        _gates_kernel, grid=(bsz, n // tm),
        in_specs=[pl.BlockSpec((1, tm, k), lambda b, t: (b, t, 0)),
                  pl.BlockSpec((None, LANES, k), lambda b, t: (layer, rb, 0)),
                  pl.BlockSpec((None, 1, LANES), lambda b, t: (layer, 0, rb))],
        out_specs=pl.BlockSpec((1, tm, LANES), lambda b, t: (b, t, 0)),
        out_shape=jax.ShapeDtypeStruct((bsz, n, LANES), F32),
        compiler_params=_params(("arbitrary", "arbitrary"), 40),
        name="gate_proj",
    )(x, w_t, bias)


def _mm_resid_kernel(x_ref, w_ref, z_ref, gctx_ref, glat_ref, o_ref, *, n_ctx):
    acc = _dot(x_ref[0].astype(BF16), w_ref[...].astype(BF16))
    tm = acc.shape[0]
    row = pl.program_id(2) * tm + lax.broadcasted_iota(jnp.int32, (tm, 1), 0)
    gate = jnp.where(row < n_ctx, gctx_ref[...], glat_ref[...])
    o_ref[0] = z_ref[0] + gate * acc


def matmul_resid(x, w, z, mods, layer, gate_chunk, n_ctx):
    bsz, n, k = x.shape
    nc = w.shape[-1]
    tm, tn = 768, 512
    gate_spec = lambda ctx: pl.BlockSpec(
        (None, None, 1, tn),
        lambda j, b, t: (layer, bsz if ctx else b, 0, gate_chunk * (nc // tn) + j))
    return pl.pallas_call(
        functools.partial(_mm_resid_kernel, n_ctx=n_ctx), grid=(nc // tn, bsz, n // tm),
        in_specs=[pl.BlockSpec((1, tm, k), lambda j, b, t: (b, t, 0)),
                  pl.BlockSpec((None, k, tn), lambda j, b, t: (layer, 0, j)),
                  pl.BlockSpec((1, tm, tn), lambda j, b, t: (b, t, j)),
                  gate_spec(True), gate_spec(False)],
        out_specs=pl.BlockSpec((1, tm, tn), lambda j, b, t: (b, t, j)),
        out_shape=jax.ShapeDtypeStruct((bsz, n, nc), F32),
        compiler_params=_params(("arbitrary", "arbitrary", "arbitrary"), 48),
        name="matmul_resid",
    )(x, w, z, mods, mods)


def _conv_kernel(x_ref, w_ref, b_ref, o_ref, ot_ref, *, n_ctx):
    x = x_ref[0]
    n = x.shape[0]
    row = lax.broadcasted_iota(jnp.int32, x.shape, 0)
    prev = jnp.where((row == 0) | (row == n_ctx), 0.0, pltpu.roll(x, 1, 0))
    nxt = jnp.where((row == n_ctx - 1) | (row == n - 1), 0.0, pltpu.roll(x, n - 1, 0))
    w = w_ref[...]
    y = b_ref[...] + prev * w[0:1] + x * w[1:2] + nxt * w[2:3]
    y = y * jax.nn.sigmoid(y)
    k_tiles = pl.num_programs(1) // 2
    y = y * jnp.where(pl.program_id(1) >= k_tiles, ML_DH ** -0.5, 1.0)
    o_ref[0] = y.astype(o_ref.dtype)
    ot_ref[0] = y.T.astype(ot_ref.dtype)


def conv_silu(proj, conv_w, conv_b, layer, n_ctx):
    bsz, n, _ = proj.shape
    tc = 256
    c2 = 2 * MIX_W
    return pl.pallas_call(
        functools.partial(_conv_kernel, n_ctx=n_ctx),
        grid=(bsz, c2 // tc),
        in_specs=[pl.BlockSpec((1, n, tc), lambda b, c: (b, 0, c)),
                  pl.BlockSpec((None, conv_w.shape[1], tc), lambda b, c: (layer, 0, c)),
                  pl.BlockSpec((None, 1, tc), lambda b, c: (layer, 0, c))],
        out_specs=[pl.BlockSpec((1, n, tc), lambda b, c: (b, 0, c)),
                   pl.BlockSpec((1, tc, n), lambda b, c: (b, c, 0))],
        out_shape=[jax.ShapeDtypeStruct((bsz, n, c2), BF16),
                   jax.ShapeDtypeStruct((bsz, c2, n), BF16)],
        compiler_params=_params(("arbitrary", "arbitrary"), 48),
        name="conv_silu",
    )(proj, conv_w, conv_b.reshape(conv_b.shape[0], 1, c2))


def _log_sigmoid(x):
    return jnp.minimum(x, 0.0) - jnp.log(1.0 + jnp.exp(-jnp.abs(x)))


def _mlstm_kernel(qf_ref, kf_ref, ktf_ref, vf_ref, gf_ref, gtf_ref,
                  qb_ref, kb_ref, ktb_ref, vb_ref, gb_ref, gtb_ref,
                  hf_ref, hb_ref, c_ref, m_ref):
    t = qf_ref.shape[1]
    dh = ML_DH

    @pl.when(pl.program_id(1) == 0)
    def _():
        c_ref[...] = jnp.zeros_like(c_ref)
        m_ref[...] = jnp.zeros_like(m_ref)

    r_i = lax.broadcasted_iota(jnp.int32, (t, t), 0)
    c_i = lax.broadcasted_iota(jnp.int32, (t, t), 1)
    low = c_i <= r_i
    upp = c_i >= r_i
    low_b = low.astype(BF16)
    upp_b = upp.astype(BF16)
    ones_col = (lax.broadcasted_iota(jnp.int32, (t, LANES), 1) == 0).astype(F32)

    dirs = ((qf_ref, kf_ref, ktf_ref, vf_ref, gf_ref, gtf_ref, hf_ref, low, low_b, upp_b),
            (qb_ref, kb_ref, ktb_ref, vb_ref, gb_ref, gtb_ref, hb_ref, upp, upp_b, low_b))
    for d, (q_ref, k_ref, kt_ref, v_ref, g_ref, gt_ref, h_ref, mask, col_tri, row_tri) in enumerate(dirs):
        g = g_ref[0]
        gt = gt_ref[0]
        lf = _log_sigmoid(g)
        lft = _log_sigmoid(gt)
        b_cols = sum(_dot(col_tri, p) for p in _split3(lf))
        b_rows = sum(_dot(p, row_tri) for p in _split3(lft))
        last = t - 1 if d == 0 else 0
        for hd in range(ML_HEADS):
            ci = 2 * d * ML_HEADS + hd
            cf = ci + ML_HEADS
            idx = d * ML_HEADS + hd
            sl = slice(hd * dh, (hd + 1) * dh)
            q = q_ref[0, :, sl]
            k = k_ref[0, :, sl]
            kt = kt_ref[0, sl, :]
            v = v_ref[0, :, sl]
            i_col = g[:, ci:ci + 1]
            i_row = gt[ci:ci + 1, :]
            b_col = b_cols[:, cf:cf + 1]
            b_row = b_rows[cf:cf + 1, :]
            b_tot = b_col[last:last + 1, :]
            m_prev = m_ref[idx][:, :1]
            c_prev = c_ref[idx]

            log_inter = b_col + m_prev
            log_intra = jnp.where(mask, b_col - b_row + i_row, -jnp.inf)
            m_t = jnp.maximum(log_inter, jnp.max(log_intra, axis=-1, keepdims=True))
            w_inter = jnp.exp(log_inter - m_t)
            s = _dot_nt(q, k) * jnp.exp(log_intra - m_t)
            v_ext = jnp.concatenate([v, ones_col], axis=1)
            nd = w_inter * _dot(q, c_prev.astype(BF16)) + _dot(s.astype(BF16), v_ext.astype(BF16))
            den = nd[:, dh:dh + 1]
            h_ref[0, :, sl] = nd[:, :dh] * (1.0 / jnp.maximum(jnp.abs(den), jnp.exp(-m_t)))

            log_w = b_tot - b_col + i_col
            m_new = jnp.maximum(b_tot + m_prev, jnp.max(log_w, axis=0, keepdims=True))
            decay = jnp.exp(b_tot + m_prev - m_new)
            w_s = jnp.exp(log_w - m_new)
            c_ref[idx] = decay * c_prev + _dot(kt, (v_ext * w_s).astype(BF16))
            m_ref[idx] = jnp.broadcast_to(m_new, (1, LANES))


def mlstm(qk, qkt, proj, gates, gates_t, n_ctx):
    bsz, n, _ = qk.shape
    t = SEQ_CHUNK
    nc = n // t
    c0 = n_ctx // t
    w = MIX_W
    fwd = lambda j: j
    bwd = lambda j: _rev_chunk(j, c0, nc)

    def specs(ch):
        return [pl.BlockSpec((1, t, w), lambda b, j: (b, ch(j), 0)),
                pl.BlockSpec((1, t, w), lambda b, j: (b, ch(j), 1)),
                pl.BlockSpec((1, w, t), lambda b, j: (b, 1, ch(j))),
                pl.BlockSpec((1, t, w), lambda b, j: (b, ch(j), COL_VA // w)),
                pl.BlockSpec((1, t, LANES), lambda b, j: (b, ch(j), 0)),
                pl.BlockSpec((1, N_GATES, t), lambda b, j: (b, 0, ch(j)))]

    args = [qk, qk, qkt, proj, gates, gates_t]
    out = jax.ShapeDtypeStruct((bsz, n, w), F32)
    return pl.pallas_call(
        _mlstm_kernel, grid=(bsz, nc),
        in_specs=specs(fwd) + specs(bwd),
        out_specs=[pl.BlockSpec((1, t, w), lambda b, j: (b, fwd(j), 0)),
                   pl.BlockSpec((1, t, w), lambda b, j: (b, bwd(j), 0))],
        out_shape=[out, out],
        scratch_shapes=[pltpu.VMEM((2 * ML_HEADS, ML_DH, ML_DH + LANES), F32),
                        pltpu.VMEM((2 * ML_HEADS, 1, LANES), F32)],
        compiler_params=_params(("arbitrary", "arbitrary"), 48),
        name="mlstm",
    )(*(args + args))


def _ya_kernel(hf_ref, hb_ref, oa_ref, g_ref, o_ref):
    h = hf_ref[0] + hb_ref[0]
    parts = []
    for hd in range(ML_HEADS):
        x = h[:, hd * ML_DH:(hd + 1) * ML_DH]
        parts.append(x * lax.rsqrt(jnp.mean(x * x, axis=-1, keepdims=True) + EPS))
    y = jnp.concatenate(parts, axis=1) * g_ref[...]
    o_ref[0] = (jax.nn.sigmoid(oa_ref[0]) * y).astype(o_ref.dtype)


def mlstm_out(hf, hb, proj, norm_g, layer):
    bsz, n, w = hf.shape
    tm = SEQ_CHUNK
    tile = pl.BlockSpec((1, tm, w), lambda b, t: (b, t, 0))
    return pl.pallas_call(
        _ya_kernel, grid=(bsz, n // tm),
        in_specs=[tile, tile, pl.BlockSpec((1, tm, w), lambda b, t: (b, t, COL_OA // w)),
                  pl.BlockSpec((None, 1, w), lambda b, t: (layer, 0, 0))],
        out_specs=tile, out_shape=jax.ShapeDtypeStruct((bsz, n, w), BF16),
        compiler_params=_params(("arbitrary", "arbitrary"), 40),
        name="mlstm_out",
    )(hf, hb, proj, norm_g.reshape(norm_g.shape[0], 1, w))


def _s5_kernel(uf_ref, ub_ref, bm_ref, cm_ref, a_ref, yf_ref, yb_ref, lhs_ref, re_ref, im_ref, y_ref, carry_ref):
    nb, t, _ = uf_ref.shape
    rows = t * nb
    sw = S5_BUNDLE * S5_STATE
    tiles_per_bundle = sw // LANES
    in_slabs = MIX_W // LANES

    @pl.when(pl.program_id(0) == 0)
    def _():
        carry_ref[...] = jnp.zeros_like(carry_ref)

    for d, u_ref in enumerate((uf_ref, ub_ref)):
        for b in range(nb):
            for s in range(in_slabs):
                lhs_ref[d, s, pl.ds(b, t, stride=nb), :] = u_ref[b, :, s * LANES:(s + 1) * LANES]
        for kb in range(S5_NB):
            lhs = jnp.concatenate([lhs_ref[d, 2 * kb], lhs_ref[d, 2 * kb + 1]], axis=1).astype(BF16)
            bu = _dot(lhs, bm_ref[d, kb])
            half = kb // (S5_NB // S5_HALVES)
            for lt in range(tiles_per_bundle):
                s = (kb % (S5_NB // S5_HALVES)) * tiles_per_bundle + lt
                re_ref[d, s, pl.ds(half, rows, stride=S5_HALVES), :] = bu[:, lt * LANES:(lt + 1) * LANES]
                im_ref[d, s, pl.ds(half, rows, stride=S5_HALVES), :] = bu[:, sw + lt * LANES:sw + (lt + 1) * LANES]

    for g in range(S5_SLABS // S5_GRP):
        slabs = tuple(range(g * S5_GRP, (g + 1) * S5_GRP))
        a_vals = [[(a_ref[d, 0, s], a_ref[d, 1, s]) for s in slabs] for d in range(2)]
        init = tuple(carry_ref[d, c, s] for d in range(2) for s in slabs for c in range(2))

        def body(i, xs, slabs=slabs, a_vals=a_vals):
            out = []
            for d in range(2):
                tok = i if d == 0 else t - 1 - i
                r0 = pl.multiple_of(tok * SUBLANES, SUBLANES)
                for j, s in enumerate(slabs):
                    xr, xi = xs[2 * (d * S5_GRP + j)], xs[2 * (d * S5_GRP + j) + 1]
                    ar, ai = a_vals[d][j]
                    nr = ar * xr - ai * xi + re_ref[d, s, pl.ds(r0, SUBLANES), :]
                    ni = ar * xi + ai * xr + im_ref[d, s, pl.ds(r0, SUBLANES), :]
                    re_ref[d, s, pl.ds(r0, SUBLANES), :] = nr
                    im_ref[d, s, pl.ds(r0, SUBLANES), :] = ni
                    out += [nr, ni]
            return tuple(out)

        fin = lax.fori_loop(0, t, body, init, unroll=2)
        for d in range(2):
            for j, s in enumerate(slabs):
                for c in range(2):
                    carry_ref[d, c, s] = fin[2 * (d * S5_GRP + j) + c]

    for d, out_ref in enumerate((yf_ref, yb_ref)):
        for kb in range(S5_NB):
            half = kb // (S5_NB // S5_HALVES)
            s0 = (kb % (S5_NB // S5_HALVES)) * tiles_per_bundle
            sr = jnp.concatenate([re_ref[d, s0 + lt, pl.ds(half, rows, stride=S5_HALVES), :]
                                  for lt in range(tiles_per_bundle)], axis=1).astype(BF16)
            si = jnp.concatenate([im_ref[d, s0 + lt, pl.ds(half, rows, stride=S5_HALVES), :]
                                  for lt in range(tiles_per_bundle)], axis=1).astype(BF16)
            y = _dot(sr, cm_ref[kb, 0]) + _dot(si, cm_ref[kb, 1])
            y_ref[d, 2 * kb] = y[:, :LANES]
            y_ref[d, 2 * kb + 1] = y[:, LANES:]
        for b in range(nb):
            for s in range(in_slabs):
                out_ref[b, :, s * LANES:(s + 1) * LANES] = y_ref[d, s, pl.ds(b, t, stride=nb), :]


def s5_scan(proj, bmat, cmat, atab, n_ctx):
    bsz, n, _ = proj.shape
    assert bsz * S5_HALVES == SUBLANES, "the scan layout puts (sample, state half) on the 8 sublanes"
    t = S5_TC
    nc = n // t
    c0 = n_ctx // t
    w = MIX_W
    rows = t * bsz
    bwd = lambda j: _rev_chunk(j, c0, nc)
    out = jax.ShapeDtypeStruct((bsz, n, w), F32)
    const = lambda a: pl.BlockSpec(a.shape, lambda j: (0,) * a.ndim)
    return pl.pallas_call(
        _s5_kernel, grid=(nc,),
        in_specs=[pl.BlockSpec((bsz, t, w), lambda j: (0, j, COL_US // w)),
                  pl.BlockSpec((bsz, t, w), lambda j: (0, bwd(j), COL_US // w)),
                  const(bmat), const(cmat), const(atab)],
        out_specs=[pl.BlockSpec((bsz, t, w), lambda j: (0, j, 0)),
                   pl.BlockSpec((bsz, t, w), lambda j: (0, bwd(j), 0))],
        out_shape=[out, out],
        scratch_shapes=[pltpu.VMEM((2, w // LANES, rows, LANES), F32),
                        pltpu.VMEM((2, S5_SLABS, rows * S5_HALVES, LANES), F32),
                        pltpu.VMEM((2, S5_SLABS, rows * S5_HALVES, LANES), F32),
                        pltpu.VMEM((2, w // LANES, rows, LANES), F32),
                        pltpu.VMEM((2, 2, S5_SLABS, SUBLANES, LANES), F32)],
        compiler_params=_params(("arbitrary",), 56),
        name="s5_scan",
    )(proj, proj, bmat, cmat, atab)


def s5_operands(lam_re, lam_im, log_dt, b_re, b_im, c_re, c_im):
    lam_re, lam_im = lam_re.astype(F32), lam_im.astype(F32)
    dt = jnp.exp(log_dt.astype(F32))[..., None]
    mag = jnp.exp(lam_re * dt)
    a_re, a_im = mag * jnp.cos(lam_im * dt), mag * jnp.sin(lam_im * dt)
    nr, ni = a_re - 1.0, a_im
    den = lam_re * lam_re + lam_im * lam_im
    fr = (nr * lam_re + ni * lam_im) / den
    fi = (ni * lam_re - nr * lam_im) / den
    bb_re = fr[..., None] * b_re - fi[..., None] * b_im
    bb_im = fr[..., None] * b_im + fi[..., None] * b_re
    eye = jnp.eye(S5_BUNDLE, dtype=F32)

    def blockdiag_b(bb):
        x = bb.reshape(2, S5_NB, S5_BUNDLE, S5_STATE, S5_GROUP)
        return jnp.einsum('dkgpi,gh->dkgihp', x, eye).reshape(
            2, S5_NB, S5_BUNDLE * S5_GROUP, S5_BUNDLE * S5_STATE)

    bmat = jnp.concatenate([blockdiag_b(bb_re), blockdiag_b(bb_im)], -1).astype(BF16)

    def blockdiag_c(c):
        x = c.astype(F32).reshape(S5_NB, S5_BUNDLE, S5_GROUP, S5_STATE)
        return jnp.einsum('kgip,gh->kgphi', x, eye).reshape(
            S5_NB, S5_BUNDLE * S5_STATE, S5_BUNDLE * S5_GROUP)

    cmat = jnp.stack([blockdiag_c(c_re), -blockdiag_c(c_im)], 1).astype(BF16)

    def slab_table(a):
        a4 = a.reshape(2, S5_HALVES, S5_SLABS, LANES)
        return jnp.take(a4, jnp.arange(SUBLANES) % S5_HALVES, axis=1).transpose(0, 2, 1, 3)

    atab = jnp.stack([slab_table(a_re), slab_table(a_im)], 1)
    return bmat, cmat, atab


def _glu_kernel(yf_ref, yb_ref, u_ref, d_ref, w_ref, b_ref, o_ref):
    y = d_ref[...] * u_ref[0] + yf_ref[0] + yb_ref[0]
    z = jax.nn.gelu(y)
    o_ref[0] = (z * jax.nn.sigmoid(_dot(z.astype(BF16), w_ref[...].astype(BF16)) + b_ref[...])).astype(o_ref.dtype)


def s5_glu(yf, yb, proj, d_skip, glu_w, glu_b, layer):
    bsz, n, w = yf.shape
    tm = SEQ_CHUNK
    tile = pl.BlockSpec((1, tm, w), lambda b, t: (b, t, 0))
    vec = pl.BlockSpec((None, 1, w), lambda b, t: (layer, 0, 0))
    return pl.pallas_call(
        _glu_kernel, grid=(bsz, n // tm),
        in_specs=[tile, tile, pl.BlockSpec((1, tm, w), lambda b, t: (b, t, COL_US // w)),
                  vec, pl.BlockSpec((None, w, w), lambda b, t: (layer, 0, 0)), vec],
        out_specs=tile, out_shape=jax.ShapeDtypeStruct((bsz, n, w), BF16),
        compiler_params=_params(("arbitrary", "arbitrary"), 40),
        name="s5_glu",
    )(yf, yb, proj, d_skip.reshape(-1, 1, w), glu_w, glu_b.reshape(-1, 1, w))


def _da_prep_kernel(x_ref, g_ref, cos_ref, sin_ref, o_ref):
    cos = cos_ref[...]
    sin = sin_ref[...]
    lane = lax.broadcasted_iota(jnp.int32, cos.shape, 1)
    first = lane < DA_DQK
    quarter = DA_DQK // 4
    up = (lane & quarter) == 0
    scale = jnp.where(pl.program_id(2) == 0, DA_DQK ** -0.5 * math.log2(math.e), 1.0)
    for hd in range(DA_HEADS):
        sl = slice(hd * LANES, (hd + 1) * LANES)
        x = x_ref[0, :, sl]
        x2 = x * x
        ms0 = jnp.sum(jnp.where(first, x2, 0.0), axis=-1, keepdims=True)
        ms1 = jnp.sum(jnp.where(first, 0.0, x2), axis=-1, keepdims=True)
        ms = jnp.where(first, ms0, ms1) * (1.0 / DA_DQK)
        y = x * lax.rsqrt(ms + EPS) * g_ref[:, sl]
        partner = jnp.where(up, pltpu.roll(y, LANES - quarter, 1), pltpu.roll(y, quarter, 1))
        o_ref[0, :, sl] = ((y * cos + partner * sin) * scale).astype(o_ref.dtype)


def da_prep(proj, qn_g, kn_g, cos_t, sin_t, layer):
    bsz, n, _ = proj.shape
    tm = SEQ_CHUNK
    w = DA_HEADS * LANES
    reps = w // DA_DQK
    gains = jnp.concatenate([jnp.tile(qn_g, (1, reps)), jnp.tile(kn_g, (1, reps))], -1)
    gains = gains.reshape(gains.shape[0], 1, 2 * w)
    return pl.pallas_call(
        _da_prep_kernel, grid=(bsz, n // tm, 2),
        in_specs=[pl.BlockSpec((1, tm, w), lambda b, t, j: (b, t, COL_QD // w + j)),
                  pl.BlockSpec((None, 1, w), lambda b, t, j: (layer, 0, j)),
                  pl.BlockSpec((tm, LANES), lambda b, t, j: (t, 0)),
                  pl.BlockSpec((tm, LANES), lambda b, t, j: (t, 0))],
        out_specs=pl.BlockSpec((1, tm, w), lambda b, t, j: (b, t, j)),
        out_shape=jax.ShapeDtypeStruct((bsz, n, 2 * w), BF16),
        compiler_params=_params(("arbitrary", "arbitrary", "arbitrary"), 40),
        name="da_prep",
    )(proj, gains, cos_t, sin_t)


def rope_tables(n, n_ctx):
    lat = jnp.arange(n) - n_ctx
    rows = (lat // GRID_W).astype(F32)
    cols = (lat % GRID_W).astype(F32)
    lane = jnp.arange(LANES)
    in_comp = lane % DA_DQK
    quarter = DA_DQK // 4
    freqs = ROPE_BASE ** (-(in_comp % quarter).astype(F32) / quarter)
    pos = jnp.where((in_comp < DA_DQK // 2)[None, :], rows[:, None], cols[:, None])
    ang = pos * freqs[None, :]
    sign = jnp.where((in_comp % (2 * quarter)) < quarter, -1.0, 1.0)
    is_lat = (lat >= 0)[:, None]
    cos_t = jnp.where(is_lat, jnp.cos(ang), 1.0)
    sin_t = jnp.where(is_lat, jnp.sin(ang) * sign[None, :], 0.0)
    return cos_t.astype(F32), sin_t.astype(F32)


DA_HEADS_PER_STEP = 2


def _diff_attn_kernel(scal_ref, q_ref, k_ref, v_ref, g_ref, o_ref, *, n_ctx, ctx_tiles):
    lam = scal_ref[0]
    out_scale = scal_ref[1]
    lane = lax.broadcasted_iota(jnp.int32, (q_ref.shape[1], LANES), 1)
    first = lane < DA_DQK

    def exp_rows(s):
        e = jnp.exp2(s - jnp.max(s, axis=-1, keepdims=True))
        return e, jnp.sum(e, axis=-1, keepdims=True)

    def attend(nk):
        for hh in range(DA_HEADS_PER_STEP):
            sl = slice(hh * LANES, (hh + 1) * LANES)
            q = q_ref[0, :, sl]
            zero = jnp.zeros_like(q)
            k = k_ref[0, :nk, sl]
            e0, l0 = exp_rows(_dot_nt(jnp.where(first, q, zero), k))
            e1, l1 = exp_rows(_dot_nt(jnp.where(first, zero, q), k))
            a = e0 - (lam * l0 * (1.0 / l1)) * e1
            o = _dot(a.astype(BF16), v_ref[0, :nk, sl].astype(BF16)) * (1.0 / l0)
            o = o * lax.rsqrt(jnp.mean(o * o, axis=-1, keepdims=True) + EPS) * g_ref[...]
            o_ref[0, :, sl] = (o * out_scale).astype(o_ref.dtype)

    @pl.when(pl.program_id(2) < ctx_tiles)
    def _():
        attend(n_ctx)

    @pl.when(pl.program_id(2) >= ctx_tiles)
    def _():
        attend(k_ref.shape[1])


def diff_attn(qk_hat, proj, scal, sub_g, layer, n_ctx):
    bsz, n, _ = qk_hat.shape
    tq = SEQ_CHUNK
    w = DA_HEADS_PER_STEP * LANES
    return pl.pallas_call(
        functools.partial(_diff_attn_kernel, n_ctx=n_ctx, ctx_tiles=n_ctx // tq),
        grid=(bsz, DA_HEADS // DA_HEADS_PER_STEP, n // tq),
        in_specs=[pl.BlockSpec(memory_space=pltpu.SMEM),
                  pl.BlockSpec((1, tq, w), lambda b, h, t: (b, t, h)),
                  pl.BlockSpec((1, n, w), lambda b, h, t: (b, 0, DA_HEADS // DA_HEADS_PER_STEP + h)),
                  pl.BlockSpec((1, n, w), lambda b, h, t: (b, 0, COL_VD // w + h)),
                  pl.BlockSpec((None, 1, LANES), lambda b, h, t: (layer, 0, 0))],
        out_specs=pl.BlockSpec((1, tq, w), lambda b, h, t: (b, t, h)),
        out_shape=jax.ShapeDtypeStruct((bsz, n, DA_HEADS * DA_DV), BF16),
        compiler_params=_params(("arbitrary", "arbitrary", "arbitrary"), 48),
        name="diff_attn",
    )(scal, qk_hat, qk_hat, proj, sub_g.reshape(sub_g.shape[0], 1, DA_DV))


def _merge_kernel(ya_ref, yb_ref, yc_ref, p_ref, ga_ref, gb_ref, gc_ref, o_ref):
    acc = _dot(ya_ref[0], p_ref[0].astype(BF16)) * jax.nn.sigmoid(ga_ref[0])
    acc += _dot(yb_ref[0], p_ref[1].astype(BF16)) * jax.nn.sigmoid(gb_ref[0])
    acc += _dot(yc_ref[0], p_ref[2].astype(BF16)) * jax.nn.sigmoid(gc_ref[0])
    o_ref[0] = acc.astype(o_ref.dtype)


def branch_merge(ya, yb, yc, proj, branch_proj, layer):
    bsz, n, w = ya.shape
    d = branch_proj.shape[-1]
    tm, tn = 768, 512
    y_tile = pl.BlockSpec((1, tm, w), lambda j, b, t: (b, t, 0))
    gate = lambda r: pl.BlockSpec((1, tm, tn), lambda j, b, t: (b, t, (COL_GB + r * d) // tn + j))
    return pl.pallas_call(
        _merge_kernel, grid=(d // tn, bsz, n // tm),
        in_specs=[y_tile, y_tile, y_tile,
                  pl.BlockSpec((None, N_BRANCH, w, tn), lambda j, b, t: (layer, 0, 0, j)),
                  gate(0), gate(1), gate(2)],
        out_specs=pl.BlockSpec((1, tm, tn), lambda j, b, t: (b, t, j)),
        out_shape=jax.ShapeDtypeStruct((bsz, n, d), BF16),
        compiler_params=_params(("arbitrary", "arbitrary", "arbitrary"), 48),
        name="branch_merge",
    )(ya, yb, yc, branch_proj, proj, proj, proj)


INFO_E, INFO_POS, INFO_W = 0, 2, 4


def _router_kernel(h_ref, w_ref, b_ref, info_ref, cnt_ref):
    tm = h_ref.shape[1]

    @pl.when((pl.program_id(0) == 0) & (pl.program_id(1) == 0))
    def _():
        cnt_ref[...] = jnp.zeros_like(cnt_ref)

    h_parts = _split3(h_ref[0])
    w_parts = _split3(w_ref[...])
    logits = (_dot_nt(w_parts[0], h_parts[0]) + _dot_nt(w_parts[0], h_parts[1])
              + _dot_nt(w_parts[1], h_parts[0]))
    score = jax.nn.sigmoid(logits)
    sel = score + b_ref[...]
    rows = [sel[e:e + 1, :] for e in range(N_EXPERTS)]
    gscores = []
    for g in range(N_EXPERT_GROUPS):
        a, b, c, d = rows[g * EXPERTS_PER_GROUP:(g + 1) * EXPERTS_PER_GROUP]
        hi1, lo1 = jnp.maximum(a, b), jnp.minimum(a, b)
        hi2, lo2 = jnp.maximum(c, d), jnp.minimum(c, d)
        gscores.append(jnp.maximum(hi1, hi2) + jnp.maximum(jnp.minimum(hi1, hi2), jnp.maximum(lo1, lo2)))
    gmax = functools.reduce(jnp.maximum, gscores)
    taken = jnp.zeros_like(gmax, dtype=jnp.bool_)
    picked = []
    for g in range(N_EXPERT_GROUPS):
        is_g = (gscores[g] == gmax) & jnp.logical_not(taken)
        picked.append(is_g)
        taken = taken | is_g
    chosen = []
    for e in range(N_EXPERTS):
        g = e // EXPERTS_PER_GROUP
        rank = jnp.zeros_like(gmax)
        for o in range(g * EXPERTS_PER_GROUP, (g + 1) * EXPERTS_PER_GROUP):
            if o == e:
                continue
            ahead = (rows[o] > rows[e]) | ((rows[o] == rows[e]) & (o < e))
            rank = rank + ahead.astype(F32)
        chosen.append(picked[g] & (rank < float(TOP_K)))
    onehot = jnp.concatenate([c.astype(F32) for c in chosen], axis=0)
    total = jnp.sum(onehot * score, axis=0, keepdims=True)
    gates = onehot * score * (1.0 / total)

    r_i = lax.broadcasted_iota(jnp.int32, (tm, tm), 0)
    c_i = lax.broadcasted_iota(jnp.int32, (tm, tm), 1)
    incl = _dot(onehot.astype(BF16), (r_i <= c_i).astype(BF16))
    pos = incl - onehot + cnt_ref[:, :1]
    cnt_ref[...] = cnt_ref[...] + incl[:, tm - 1:tm]

    seen = jnp.zeros_like(gmax)
    rec = [jnp.zeros_like(gmax) for _ in range(3 * TOP_K)]
    for e in range(N_EXPERTS):
        oh = onehot[e:e + 1, :]
        for k, is_k in enumerate((oh * (1.0 - seen), oh * seen)):
            rec[INFO_E + k] += is_k * float(e)
            rec[INFO_POS + k] += is_k * pos[e:e + 1, :]
            rec[INFO_W + k] += is_k * gates[e:e + 1, :]
        seen = seen + oh
    pad = jnp.zeros((LANES - len(rec), tm), F32)
    info_ref[0] = jnp.concatenate(rec + [pad], axis=0).T


def moe_router(h2, router_w, router_b):
    bsz, n, d = h2.shape
    tm = SEQ_CHUNK
    return pl.pallas_call(
        _router_kernel, grid=(bsz, n // tm),
        in_specs=[pl.BlockSpec((1, tm, d), lambda b, t: (b, t, 0)),
                  pl.BlockSpec((N_EXPERTS, d), lambda b, t: (0, 0)),
                  pl.BlockSpec((N_EXPERTS, 1), lambda b, t: (0, 0))],
        out_specs=[pl.BlockSpec((1, tm, LANES), lambda b, t: (b, t, 0)),
                   pl.BlockSpec((N_EXPERTS, LANES), lambda b, t: (0, 0))],
        out_shape=[jax.ShapeDtypeStruct((bsz, n, LANES), F32),
                   jax.ShapeDtypeStruct((N_EXPERTS, LANES), F32)],
        compiler_params=_params(("arbitrary", "arbitrary"), 40),
        name="moe_router",
    )(h2, router_w.T, router_b.reshape(N_EXPERTS, 1))


def moe_plan(info, counts, n_tiles):
    m = info.shape[0] * info.shape[1]
    rec = info.reshape(m, LANES)
    cnt = counts[:, 0].astype(jnp.int32)
    padded = (cnt + MOE_TILE - 1) // MOE_TILE * MOE_TILE
    ends = jnp.cumsum(padded)
    starts = ends - padded
    experts = rec[:, INFO_E:INFO_E + TOP_K].astype(jnp.int32)
    dest = starts[experts] + rec[:, INFO_POS:INFO_POS + TOP_K].astype(jnp.int32)
    tile_start = jnp.arange(n_tiles, dtype=jnp.int32) * MOE_TILE
    tile_expert = jnp.minimum(jnp.sum(ends[None, :] <= tile_start[:, None], axis=1), N_EXPERTS - 1).astype(jnp.int32)
    n_used = (ends[-1] // MOE_TILE).astype(jnp.int32).reshape(1)
    return dest.reshape(-1), tile_expert, n_used


def _row_copy(src_ref, src_row, dst_ref, dst_row, sem):
    return pltpu.make_async_copy(src_ref.at[pl.ds(src_row, 1), :], dst_ref.at[pl.ds(dst_row, 1), :], sem)


def _dispatch_kernel(dest_ref, h_ref, xs_in_ref, xs_ref, sem):
    del xs_in_ref
    tm = h_ref.shape[0]
    base = pl.program_id(0) * tm

    def start(r, carry):
        for k in range(TOP_K):
            _row_copy(h_ref, r, xs_ref, dest_ref[(base + r) * TOP_K + k], sem).start()
        return carry

    def wait(r, carry):
        for k in range(TOP_K):
            _row_copy(h_ref, 0, xs_ref, 0, sem).wait()
        return carry

    lax.fori_loop(0, tm, start, 0, unroll=DMA_UNROLL)
    lax.fori_loop(0, tm, wait, 0, unroll=DMA_UNROLL)


def moe_dispatch(h2, dest, xs_init):
    m, d = h2.shape
    n_rows = xs_init.shape[0]
    tm = SEQ_CHUNK
    grid_spec = pltpu.PrefetchScalarGridSpec(
        num_scalar_prefetch=1, grid=(m // tm,),
        in_specs=[pl.BlockSpec((tm, d), lambda i, dest: (i, 0)),
                  pl.BlockSpec(memory_space=pl.ANY)],
        out_specs=pl.BlockSpec(memory_space=pl.ANY),
        scratch_shapes=[pltpu.SemaphoreType.DMA])
    return pl.pallas_call(
        _dispatch_kernel, grid_spec=grid_spec,
        out_shape=jax.ShapeDtypeStruct((n_rows, d), h2.dtype),
        input_output_aliases={2: 0},
        compiler_params=_params(("arbitrary",), 40),
        name="moe_dispatch",
    )(dest, h2, xs_init)


def _expert_kernel(te_ref, nu_ref, x_ref, wg_ref, wu_ref, wd_ref, o_ref, wg_scr, wu_scr, wd_scr):
    i = pl.program_id(0)
    used = i < nu_ref[0]

    @pl.when(used & ((i == 0) | (te_ref[i] != te_ref[jnp.maximum(i - 1, 0)])))
    def _():
        wg_scr[...] = wg_ref[...].astype(BF16)
        wu_scr[...] = wu_ref[...].astype(BF16)
        wd_scr[...] = wd_ref[...].astype(BF16)

    @pl.when(used)
    def _():
        x = x_ref[...].astype(BF16)
        a = _dot(x, wg_scr[...])
        act = (a * jax.nn.sigmoid(a)) * _dot(x, wu_scr[...])
        o_ref[...] = _dot(act.astype(BF16), wd_scr[...])

    @pl.when(jnp.logical_not(used))
    def _():
        o_ref[...] = jnp.zeros_like(o_ref)


def moe_experts(xs, tile_expert, n_used, w_gate, w_up, w_down, layer):
    n_rows, d = xs.shape
    f = w_gate.shape[-1]
    w_spec = lambda r, c: pl.BlockSpec((None, None, r, c), lambda i, te, nu: (layer, te[i], 0, 0))
    grid_spec = pltpu.PrefetchScalarGridSpec(
        num_scalar_prefetch=2, grid=(n_rows // MOE_TILE,),
        in_specs=[pl.BlockSpec((MOE_TILE, d), lambda i, te, nu: (i, 0)),
                  w_spec(d, f), w_spec(d, f), w_spec(f, d)],
        out_specs=pl.BlockSpec((MOE_TILE, d), lambda i, te, nu: (i, 0)),
        scratch_shapes=[pltpu.VMEM((d, f), BF16), pltpu.VMEM((d, f), BF16), pltpu.VMEM((f, d), BF16)])
    return pl.pallas_call(
        _expert_kernel, grid_spec=grid_spec,
        out_shape=jax.ShapeDtypeStruct((n_rows, d), F32),
        compiler_params=_params(("arbitrary",), 56),
        name="moe_experts",
    )(tile_expert, n_used, xs, w_gate, w_up, w_down)


def _combine_kernel(dest_ref, ys_ref, info_ref, z_ref, gate_ref, o_ref, buf_ref, sem):
    tm = z_ref.shape[1]
    base = (pl.program_id(0) * pl.num_programs(1) + pl.program_id(1)) * tm

    def start(r, carry):
        for k in range(TOP_K):
            _row_copy(ys_ref, dest_ref[(base + r) * TOP_K + k], buf_ref.at[k], r, sem).start()
        return carry

    def wait(r, carry):
        for k in range(TOP_K):
            _row_copy(ys_ref, 0, buf_ref.at[k], 0, sem).wait()
        return carry

    lax.fori_loop(0, tm, start, 0, unroll=DMA_UNROLL)
    lax.fori_loop(0, tm, wait, 0, unroll=DMA_UNROLL)
    info = info_ref[0]
    mix = info[:, INFO_W:INFO_W + 1] * buf_ref[0] + info[:, INFO_W + 1:INFO_W + 2] * buf_ref[1]
    o_ref[0] = z_ref[0] + gate_ref[...] * mix


def moe_combine(ys, dest, info, z, mods, layer, gate_chunk, n_ctx):
    bsz, n, d = z.shape
    tm = SEQ_CHUNK
    ctx_tiles = n_ctx // tm
    tile = pl.BlockSpec((1, tm, d), lambda b, t, dest: (b, t, 0))
    grid_spec = pltpu.PrefetchScalarGridSpec(
        num_scalar_prefetch=1, grid=(bsz, n // tm),
        in_specs=[pl.BlockSpec(memory_space=pl.ANY),
                  pl.BlockSpec((1, tm, LANES), lambda b, t, dest: (b, t, 0)),
                  tile,
                  pl.BlockSpec((None, None, 1, d),
                               lambda b, t, dest: (layer, jnp.where(t < ctx_tiles, bsz, b), 0, gate_chunk))],
        out_specs=tile,
        scratch_shapes=[pltpu.VMEM((TOP_K, tm, d), F32), pltpu.SemaphoreType.DMA])
    return pl.pallas_call(
        _combine_kernel, grid_spec=grid_spec,
        out_shape=jax.ShapeDtypeStruct(z.shape, F32),
        compiler_params=_params(("arbitrary", "arbitrary"), 40),
        name="moe_combine",
    )(dest, ys, info, z, mods)


def kernel(x, c, ctx, c_ctx, norm1_g, norm2_g, ada_w, ada_b, w_in, b_in, ml_conv_w, ml_conv_b, ml_norm_g, s5_lam_re, s5_lam_im, s5_log_dt, s5_b_re, s5_b_im, s5_c_re, s5_c_im, s5_d, s5_glu_w, s5_glu_b, da_q_norm_g, da_k_norm_g, da_lam_q1, da_lam_k1, da_lam_q2, da_lam_k2, da_sub_norm_g, branch_proj, w_out, router_w, router_b, exp_w_gate, exp_w_up, exp_w_down):
    bsz, n_lat, d = x.shape
    n_ctx = ctx.shape[1]
    n = n_ctx + n_lat
    depth = w_in.shape[0]
    assert n_ctx % SEQ_CHUNK == 0 and n_lat % SEQ_CHUNK == 0 and d == D_MODEL

    rows = -(-(bsz + 1) // SUBLANES) * SUBLANES
    cvec = jnp.zeros((rows, d), F32).at[:bsz].set(c).at[bsz].set(c_ctx)
    mods = ada_mod(cvec, ada_w, ada_b).reshape(depth, rows, 1, 6 * d)

    w_in_t = jnp.swapaxes(w_in, 1, 2)
    b_in3 = b_in[:, None, :]
    b_ml = b_in3[..., :GATE_OFF]
    b_rest = b_in3[..., REST_OFF:]
    m_tok = bsz * n
    moe_tiles = m_tok * TOP_K // MOE_TILE + N_EXPERTS
    xs = jnp.zeros((moe_tiles * MOE_TILE, d), F32)
    cos_t, sin_t = rope_tables(n, n_ctx)
    z = jnp.concatenate([ctx, x], 1)
    for i in range(depth):
        lam_init = 0.8 - 0.6 * math.exp(-0.3 * i)
        (h,) = norm_mod(z, norm1_g, mods, i, 0, 1, n_ctx, (BF16,))
        proj_ml = in_proj(h, w_in_t, b_ml, i, 0)
        proj = in_proj(h, w_in_t, b_rest, i, REST_OFF)
        gates = gate_proj(h, w_in_t, b_in3, i, GATE_OFF)
        gates_t = jnp.swapaxes(gates[..., :N_GATES], 1, 2)

        qk, qkt = conv_silu(proj_ml, ml_conv_w, ml_conv_b, i, n_ctx)
        hf, hb = mlstm(qk, qkt, proj_ml, gates, gates_t, n_ctx)
        ya = mlstm_out(hf, hb, proj_ml, ml_norm_g, i)

        bmat, cmat, atab = s5_operands(s5_lam_re[i], s5_lam_im[i], s5_log_dt[i], s5_b_re[i], s5_b_im[i],
                                       s5_c_re[i], s5_c_im[i])
        yf, ybk = s5_scan(proj, bmat, cmat, atab, n_ctx)
        yb = s5_glu(yf, ybk, proj, s5_d, s5_glu_w, s5_glu_b, i)

        qk_hat = da_prep(proj, da_q_norm_g, da_k_norm_g, cos_t, sin_t, i)
        lam = (jnp.exp(jnp.sum(da_lam_q1[i].astype(F32) * da_lam_k1[i].astype(F32)))
               - jnp.exp(jnp.sum(da_lam_q2[i].astype(F32) * da_lam_k2[i].astype(F32))) + lam_init)
        scal = jnp.stack([lam, jnp.asarray(1.0 - lam_init, F32)]).astype(F32)
        yc = diff_attn(qk_hat, proj, scal, da_sub_norm_g, i, n_ctx)

        merged = branch_merge(ya, yb, yc, proj, branch_proj, i)
        z = matmul_resid(merged, w_out, z, mods, i, 2, n_ctx)

        (h2,) = norm_mod(z, norm2_g, mods, i, 3, 4, n_ctx, (F32,))
        info, counts = moe_router(h2, router_w, router_b)
        dest, tile_expert, n_used = moe_plan(info, counts, moe_tiles)
        xs = moe_dispatch(h2.reshape(m_tok, d), dest, xs)
        ys = moe_experts(xs, tile_expert, n_used, exp_w_gate, exp_w_up, exp_w_down, i)
        z = moe_combine(ys, dest, info, z, mods, i, 5, n_ctx)
    return z[:, n_ctx:]
```

```python
import functools
import math

import jax
import jax.numpy as jnp
from jax import lax
from jax.experimental import pallas as pl
from jax.experimental.pallas import tpu as pltpu

F32 = jnp.float32
BF16 = jnp.bfloat16

D_MODEL = 2048
GRID_W = 64
MIX_W = 1024
N_BRANCH = 3
ML_HEADS = 4
ML_DH = MIX_W // ML_HEADS
S5_GROUP = 16
S5_GROUPS = MIX_W // S5_GROUP
S5_STATE = 64
S5_BUNDLE = 16
S5_NB = S5_GROUPS // S5_BUNDLE
S5_NS = S5_GROUPS * S5_STATE
DA_HEADS = 8
DA_DQK = 64
DA_DV = 2 * DA_DQK
ROPE_BASE = 10000.0
N_EXPERTS = 16
N_EXPERT_GROUPS = 4
EXPERTS_PER_GROUP = N_EXPERTS // N_EXPERT_GROUPS
TOP_K = 2
D_FF_EXPERT = 512
EPS = 1e-6

COL_QA, COL_KA, COL_VA, COL_OA = 0, 1024, 2048, 3072
COL_US, COL_QD, COL_KD, COL_VD, COL_GB = 0, 1024, 2048, 3072, 4096
GATE_OFF = 4 * MIX_W
N_GATES = 4 * ML_HEADS
REST_OFF = GATE_OFF + N_GATES
REST_COLS = 4 * MIX_W + N_BRANCH * D_MODEL

SEQ_CHUNK = 256
ROW_TILE = 768
LANES = 128
SUBLANES = 8
MIB = 1024 * 1024

S5_TC = 64
S5_HALVES = 2
S5_SLABS = S5_NS // S5_HALVES // LANES
S5_GRP = 8
MOE_TILE = 512
DMA_UNROLL = 8


def _params(sem, vmem_mib):
    return pltpu.CompilerParams(dimension_semantics=sem, vmem_limit_bytes=vmem_mib * MIB)


def _dot(a, b):
    return jnp.dot(a, b, preferred_element_type=F32)


def _dot_nt(a, b):
    return lax.dot_general(a, b, (((1,), (1,)), ((), ())), preferred_element_type=F32)


def _split3(x):
    hi = x.astype(BF16)
    r = x - hi.astype(F32)
    mid = r.astype(BF16)
    lo = (r - mid.astype(F32)).astype(BF16)
    return hi, mid, lo


def _rev_chunk(j, n_ctx_chunks, n_chunks):
    return jnp.where(j < n_ctx_chunks, n_ctx_chunks - 1 - j, n_chunks - 1 - (j - n_ctx_chunks))


def _ada_kernel(c_ref, w_ref, b_ref, o_ref):
    c = c_ref[...]
    s = (c * jax.nn.sigmoid(c)).astype(BF16)
    o_ref[0] = _dot(s, w_ref[0].astype(BF16)) + b_ref[0]


def ada_mod(cvec, ada_w, ada_b):
    depth, d, n6 = ada_w.shape
    rows = cvec.shape[0]
    tn = 1024
    return pl.pallas_call(
        _ada_kernel,
        grid=(depth, n6 // tn),
        in_specs=[pl.BlockSpec((rows, d), lambda l, j: (0, 0)),
                  pl.BlockSpec((1, d, tn), lambda l, j: (l, 0, j)),
                  pl.BlockSpec((1, 1, tn), lambda l, j: (l, 0, j))],
        out_specs=pl.BlockSpec((1, rows, tn), lambda l, j: (l, 0, j)),
        out_shape=jax.ShapeDtypeStruct((depth, rows, n6), F32),
        compiler_params=_params(("arbitrary", "arbitrary"), 40),
        name="ada_mod",
    )(cvec, ada_w, ada_b.reshape(depth, 1, n6))


def _mod_spec(layer, chunk, batch, ctx_tiles):
    return pl.BlockSpec((None, None, 1, D_MODEL),
                        lambda b, t: (layer, jnp.where(t < ctx_tiles, batch, b), 0, chunk))


def _norm_mod_kernel(z_ref, g_ref, shift_ref, scale_ref, *outs):
    z = z_ref[0]
    y = z * lax.rsqrt(jnp.mean(z * z, axis=-1, keepdims=True) + EPS) * g_ref[...]
    h = y * (1.0 + scale_ref[...]) + shift_ref[...]
    for o in outs:
        o[0] = h.astype(o.dtype)


def norm_mod(z, norm_g, mods, layer, shift_chunk, scale_chunk, n_ctx, out_dtypes):
    bsz, n, d = z.shape
    tm = SEQ_CHUNK
    ctx_tiles = n_ctx // tm
    tile = pl.BlockSpec((1, tm, d), lambda b, t: (b, t, 0))
    return pl.pallas_call(
        _norm_mod_kernel, grid=(bsz, n // tm),
        in_specs=[tile, pl.BlockSpec((None, 1, d), lambda b, t: (layer, 0, 0)),
                  _mod_spec(layer, shift_chunk, bsz, ctx_tiles), _mod_spec(layer, scale_chunk, bsz, ctx_tiles)],
        out_specs=[tile] * len(out_dtypes),
        out_shape=[jax.ShapeDtypeStruct(z.shape, dt) for dt in out_dtypes],
        compiler_params=_params(("arbitrary", "arbitrary"), 40),
        name="norm_mod",
    )(z, norm_g.reshape(norm_g.shape[0], 1, d), mods, mods)


def _in_proj_kernel(x_ref, wt_ref, b_ref, o_ref, w_scr):
    @pl.when((pl.program_id(1) == 0) & (pl.program_id(2) == 0))
    def _():
        rows = 256
        for r in range(0, wt_ref.shape[0], rows):
            w_scr[:, r:r + rows] = wt_ref[r:r + rows, :].T.astype(BF16)

    o_ref[0] = (_dot(x_ref[0], w_scr[...]) + b_ref[...]).astype(o_ref.dtype)


def in_proj(x, w_t, bias, layer, row0, out_dtype=F32):
    bsz, n, k = x.shape
    n_cols = bias.shape[-1]
    tm, tn = n // 2, 1024
    return pl.pallas_call(
        _in_proj_kernel, grid=(n_cols // tn, bsz, n // tm),
        in_specs=[pl.BlockSpec((1, tm, k), lambda j, b, t: (b, t, 0)),
                  pl.BlockSpec((None, pl.Element(tn), pl.Element(k)), lambda j, b, t: (layer, pl.multiple_of(row0 + j * tn, SUBLANES), 0)),
                  pl.BlockSpec((None, 1, tn), lambda j, b, t: (layer, 0, j))],
        out_specs=pl.BlockSpec((1, tm, tn), lambda j, b, t: (b, t, j)),
        out_shape=jax.ShapeDtypeStruct((bsz, n, n_cols), out_dtype),
        scratch_shapes=[pltpu.VMEM((k, tn), BF16)],
        compiler_params=_params(("arbitrary", "arbitrary", "arbitrary"), 56),
        name="in_proj",
    )(x, w_t, bias)


def _gates_kernel(x_ref, wt_ref, b_ref, o_ref):
    o_ref[0] = _dot_nt(x_ref[0], wt_ref[...].astype(BF16)) + b_ref[...]


def gate_proj(x, w_t, bias, layer, row0):
    bsz, n, k = x.shape
    tm = n // 2
    rb = row0 // LANES
    return pl.pallas_call(
        _gates_kernel, grid=(bsz, n // tm),
        in_specs=[pl.BlockSpec((1, tm, k), lambda b, t: (b, t, 0)),
                  pl.BlockSpec((None, LANES, k), lambda b, t: (layer, rb, 0)),
                  pl.BlockSpec((None, 1, LANES), lambda b, t: (layer, 0, rb))],
        out_specs=pl.BlockSpec((1, tm, LANES), lambda b, t: (b, t, 0)),
        out_shape=jax.ShapeDtypeStruct((bsz, n, LANES), F32),
        compiler_params=_params(("arbitrary", "arbitrary"), 40),
        name="gate_proj",
    )(x, w_t, bias)


def _mm_resid_kernel(x_ref, w_ref, z_ref, gctx_ref, glat_ref, o_ref, *, n_ctx):
    acc = _dot(x_ref[0].astype(BF16), w_ref[...].astype(BF16))
    tm = acc.shape[0]
    row = pl.program_id(2) * tm + lax.broadcasted_iota(jnp.int32, (tm, 1), 0)
    gate = jnp.where(row < n_ctx, gctx_ref[...], glat_ref[...])
    o_ref[0] = z_ref[0] + gate * acc


def matmul_resid(x, w, z, mods, layer, gate_chunk, n_ctx):
    bsz, n, k = x.shape
    nc = w.shape[-1]
    tm, tn = 768, 512
    gate_spec = lambda ctx: pl.BlockSpec(
        (None, None, 1, tn),
        lambda j, b, t: (layer, bsz if ctx else b, 0, gate_chunk * (nc // tn) + j))
    return pl.pallas_call(
        functools.partial(_mm_resid_kernel, n_ctx=n_ctx), grid=(nc // tn, bsz, n // tm),
        in_specs=[pl.BlockSpec((1, tm, k), lambda j, b, t: (b, t, 0)),
                  pl.BlockSpec((None, k, tn), lambda j, b, t: (layer, 0, j)),
                  pl.BlockSpec((1, tm, tn), lambda j, b, t: (b, t, j)),
                  gate_spec(True), gate_spec(False)],
        out_specs=pl.BlockSpec((1, tm, tn), lambda j, b, t: (b, t, j)),
        out_shape=jax.ShapeDtypeStruct((bsz, n, nc), F32),
        compiler_params=_params(("arbitrary", "arbitrary", "arbitrary"), 48),
        name="matmul_resid",
    )(x, w, z, mods, mods)


def _conv_kernel(x_ref, w_ref, b_ref, o_ref, ot_ref, *, n_ctx):
    x = x_ref[0]
    n = x.shape[0]
    row = lax.broadcasted_iota(jnp.int32, x.shape, 0)
    prev = jnp.where((row == 0) | (row == n_ctx), 0.0, pltpu.roll(x, 1, 0))
    nxt = jnp.where((row == n_ctx - 1) | (row == n - 1), 0.0, pltpu.roll(x, n - 1, 0))
    w = w_ref[...]
    y = b_ref[...] + prev * w[0:1] + x * w[1:2] + nxt * w[2:3]
    y = y * jax.nn.sigmoid(y)
    k_tiles = pl.num_programs(1) // 2
    y = y * jnp.where(pl.program_id(1) >= k_tiles, ML_DH ** -0.5, 1.0)
    o_ref[0] = y.astype(o_ref.dtype)
    ot_ref[0] = y.T.astype(ot_ref.dtype)


def conv_silu(proj, conv_w, conv_b, layer, n_ctx):
    bsz, n, _ = proj.shape
    tc = 256
    c2 = 2 * MIX_W
    return pl.pallas_call(
        functools.partial(_conv_kernel, n_ctx=n_ctx),
        grid=(bsz, c2 // tc),
        in_specs=[pl.BlockSpec((1, n, tc), lambda b, c: (b, 0, c)),
                  pl.BlockSpec((None, conv_w.shape[1], tc), lambda b, c: (layer, 0, c)),
                  pl.BlockSpec((None, 1, tc), lambda b, c: (layer, 0, c))],
        out_specs=[pl.BlockSpec((1, n, tc), lambda b, c: (b, 0, c)),
                   pl.BlockSpec((1, tc, n), lambda b, c: (b, c, 0))],
        out_shape=[jax.ShapeDtypeStruct((bsz, n, c2), BF16),
                   jax.ShapeDtypeStruct((bsz, c2, n), BF16)],
        compiler_params=_params(("arbitrary", "arbitrary"), 48),
        name="conv_silu",
    )(proj, conv_w, conv_b.reshape(conv_b.shape[0], 1, c2))


def _log_sigmoid(x):
    return jnp.minimum(x, 0.0) - jnp.log(1.0 + jnp.exp(-jnp.abs(x)))


def _mlstm_kernel(qf_ref, kf_ref, ktf_ref, vf_ref, gf_ref, gtf_ref,
                  qb_ref, kb_ref, ktb_ref, vb_ref, gb_ref, gtb_ref,
                  hf_ref, hb_ref, c_ref, m_ref):
    t = qf_ref.shape[1]
    dh = ML_DH

    @pl.when(pl.program_id(1) == 0)
    def _():
        c_ref[...] = jnp.zeros_like(c_ref)
        m_ref[...] = jnp.zeros_like(m_ref)

    r_i = lax.broadcasted_iota(jnp.int32, (t, t), 0)
    c_i = lax.broadcasted_iota(jnp.int32, (t, t), 1)
    low = c_i <= r_i
    upp = c_i >= r_i
    low_b = low.astype(BF16)
    upp_b = upp.astype(BF16)
    ones_col = (lax.broadcasted_iota(jnp.int32, (t, LANES), 1) == 0).astype(F32)

    dirs = ((qf_ref, kf_ref, ktf_ref, vf_ref, gf_ref, gtf_ref, hf_ref, low, low_b, upp_b),
            (qb_ref, kb_ref, ktb_ref, vb_ref, gb_ref, gtb_ref, hb_ref, upp, upp_b, low_b))
    for d, (q_ref, k_ref, kt_ref, v_ref, g_ref, gt_ref, h_ref, mask, col_tri, row_tri) in enumerate(dirs):
        g = g_ref[0]
        gt = gt_ref[0]
        lf = _log_sigmoid(g)
        lft = _log_sigmoid(gt)
        b_cols = sum(_dot(col_tri, p) for p in _split3(lf))
        b_rows = sum(_dot(p, row_tri) for p in _split3(lft))
        last = t - 1 if d == 0 else 0
        for hd in range(ML_HEADS):
            ci = 2 * d * ML_HEADS + hd
            cf = ci + ML_HEADS
            idx = d * ML_HEADS + hd
            sl = slice(hd * dh, (hd + 1) * dh)
            q = q_ref[0, :, sl]
            k = k_ref[0, :, sl]
            kt = kt_ref[0, sl, :]
            v = v_ref[0, :, sl]
            i_col = g[:, ci:ci + 1]
            i_row = gt[ci:ci + 1, :]
            b_col = b_cols[:, cf:cf + 1]
            b_row = b_rows[cf:cf + 1, :]
            b_tot = b_col[last:last + 1, :]
            m_prev = m_ref[idx][:, :1]
            c_prev = c_ref[idx]

            log_inter = b_col + m_prev
            log_intra = jnp.where(mask, b_col - b_row + i_row, -jnp.inf)
            m_t = jnp.maximum(log_inter, jnp.max(log_intra, axis=-1, keepdims=True))
            w_inter = jnp.exp(log_inter - m_t)
            s = _dot_nt(q, k) * jnp.exp(log_intra - m_t)
            v_ext = jnp.concatenate([v, ones_col], axis=1)
            nd = w_inter * _dot(q, c_prev.astype(BF16)) + _dot(s.astype(BF16), v_ext.astype(BF16))
            den = nd[:, dh:dh + 1]
            h_ref[0, :, sl] = nd[:, :dh] * (1.0 / jnp.maximum(jnp.abs(den), jnp.exp(-m_t)))

            log_w = b_tot - b_col + i_col
            m_new = jnp.maximum(b_tot + m_prev, jnp.max(log_w, axis=0, keepdims=True))
            decay = jnp.exp(b_tot + m_prev - m_new)
            w_s = jnp.exp(log_w - m_new)
            c_ref[idx] = decay * c_prev + _dot(kt, (v_ext * w_s).astype(BF16))
            m_ref[idx] = jnp.broadcast_to(m_new, (1, LANES))


def mlstm(qk, qkt, proj, gates, gates_t, n_ctx):
    bsz, n, _ = qk.shape
    t = SEQ_CHUNK
    nc = n // t
    c0 = n_ctx // t
    w = MIX_W
    fwd = lambda j: j
    bwd = lambda j: _rev_chunk(j, c0, nc)

    def specs(ch):
        return [pl.BlockSpec((1, t, w), lambda b, j: (b, ch(j), 0)),
                pl.BlockSpec((1, t, w), lambda b, j: (b, ch(j), 1)),
                pl.BlockSpec((1, w, t), lambda b, j: (b, 1, ch(j))),
                pl.BlockSpec((1, t, w), lambda b, j: (b, ch(j), COL_VA // w)),
                pl.BlockSpec((1, t, LANES), lambda b, j: (b, ch(j), 0)),
                pl.BlockSpec((1, N_GATES, t), lambda b, j: (b, 0, ch(j)))]

    args = [qk, qk, qkt, proj, gates, gates_t]
    out = jax.ShapeDtypeStruct((bsz, n, w), F32)
    return pl.pallas_call(
        _mlstm_kernel, grid=(bsz, nc),
        in_specs=specs(fwd) + specs(bwd),
        out_specs=[pl.BlockSpec((1, t, w), lambda b, j: (b, fwd(j), 0)),
                   pl.BlockSpec((1, t, w), lambda b, j: (b, bwd(j), 0))],
        out_shape=[out, out],
        scratch_shapes=[pltpu.VMEM((2 * ML_HEADS, ML_DH, ML_DH + LANES), F32),
                        pltpu.VMEM((2 * ML_HEADS, 1, LANES), F32)],
        compiler_params=_params(("arbitrary", "arbitrary"), 48),
        name="mlstm",
    )(*(args + args))


def _ya_kernel(hf_ref, hb_ref, oa_ref, g_ref, o_ref):
    h = hf_ref[0] + hb_ref[0]
    parts = []
    for hd in range(ML_HEADS):
        x = h[:, hd * ML_DH:(hd + 1) * ML_DH]
        parts.append(x * lax.rsqrt(jnp.mean(x * x, axis=-1, keepdims=True) + EPS))
    y = jnp.concatenate(parts, axis=1) * g_ref[...]
    o_ref[0] = (jax.nn.sigmoid(oa_ref[0]) * y).astype(o_ref.dtype)


def mlstm_out(hf, hb, proj, norm_g, layer):
    bsz, n, w = hf.shape
    tm = ROW_TILE
    tile = pl.BlockSpec((1, tm, w), lambda b, t: (b, t, 0))
    return pl.pallas_call(
        _ya_kernel, grid=(bsz, n // tm),
        in_specs=[tile, tile, pl.BlockSpec((1, tm, w), lambda b, t: (b, t, COL_OA // w)),
                  pl.BlockSpec((None, 1, w), lambda b, t: (layer, 0, 0))],
        out_specs=tile, out_shape=jax.ShapeDtypeStruct((bsz, n, w), BF16),
        compiler_params=_params(("arbitrary", "arbitrary"), 40),
        name="mlstm_out",
    )(hf, hb, proj, norm_g.reshape(norm_g.shape[0], 1, w))


def _s5_kernel(uf_ref, ub_ref, bm_ref, cm_ref, a_ref, yf_ref, yb_ref, lhs_ref, re_ref, im_ref, y_ref, carry_ref):
    nb, t, _ = uf_ref.shape
    rows = t * nb
    sw = S5_BUNDLE * S5_STATE
    tiles_per_bundle = sw // LANES
    in_slabs = MIX_W // LANES

    @pl.when(pl.program_id(0) == 0)
    def _():
        carry_ref[...] = jnp.zeros_like(carry_ref)

    kb_per_half = S5_NB // S5_HALVES

    def interleave_inputs(d, u_ref):
        for b in range(nb):
            for s in range(in_slabs):
                lhs_ref[d, s, pl.ds(b, t, stride=nb), :] = u_ref[b, :, s * LANES:(s + 1) * LANES]

    def b_proj(d, kb):
        lhs = jnp.concatenate([lhs_ref[d, 2 * kb], lhs_ref[d, 2 * kb + 1]], axis=1).astype(BF16)
        bu = _dot(lhs, bm_ref[d, kb])
        half = kb // kb_per_half
        for lt in range(tiles_per_bundle):
            s = (kb % kb_per_half) * tiles_per_bundle + lt
            re_ref[d, s, pl.ds(half, rows, stride=S5_HALVES), :] = bu[:, lt * LANES:(lt + 1) * LANES]
            im_ref[d, s, pl.ds(half, rows, stride=S5_HALVES), :] = bu[:, sw + lt * LANES:sw + (lt + 1) * LANES]

    def c_proj(d, kb):
        half = kb // kb_per_half
        s0 = (kb % kb_per_half) * tiles_per_bundle
        sr = jnp.concatenate([re_ref[d, s0 + lt, pl.ds(half, rows, stride=S5_HALVES), :]
                              for lt in range(tiles_per_bundle)], axis=1).astype(BF16)
        si = jnp.concatenate([im_ref[d, s0 + lt, pl.ds(half, rows, stride=S5_HALVES), :]
                              for lt in range(tiles_per_bundle)], axis=1).astype(BF16)
        y = _dot(sr, cm_ref[kb, 0]) + _dot(si, cm_ref[kb, 1])
        y_ref[d, 2 * kb] = y[:, :LANES]
        y_ref[d, 2 * kb + 1] = y[:, LANES:]

    def scan_group(g, side_work):
        slabs = tuple(range(g * S5_GRP, (g + 1) * S5_GRP))
        xs = [carry_ref[d, c, s] for d in range(2) for s in slabs for c in range(2)]
        every = t // max(len(side_work), 1)
        for i in range(t):
            if side_work and i % every == 0 and i // every < len(side_work):
                side_work[i // every]()
            for d in range(2):
                r0 = (i if d == 0 else t - 1 - i) * SUBLANES
                for j, s in enumerate(slabs):
                    k = 2 * (d * S5_GRP + j)
                    xr, xi = xs[k], xs[k + 1]
                    ar, ai = a_ref[d, 0, s], a_ref[d, 1, s]
                    nr = ar * xr - ai * xi + re_ref[d, s, r0:r0 + SUBLANES, :]
                    ni = ar * xi + ai * xr + im_ref[d, s, r0:r0 + SUBLANES, :]
                    re_ref[d, s, r0:r0 + SUBLANES, :] = nr
                    im_ref[d, s, r0:r0 + SUBLANES, :] = ni
                    xs[k], xs[k + 1] = nr, ni
        for d in range(2):
            for j, s in enumerate(slabs):
                for c in range(2):
                    carry_ref[d, c, s] = xs[2 * (d * S5_GRP + j) + c]

    assert S5_SLABS // S5_GRP == kb_per_half == 2
    bundles = lambda g: [kb for kb in range(S5_NB) if kb % kb_per_half == g]
    for d, u_ref in enumerate((uf_ref, ub_ref)):
        interleave_inputs(d, u_ref)
        for kb in bundles(0):
            b_proj(d, kb)
    scan_group(0, [functools.partial(b_proj, d, kb) for d in range(2) for kb in bundles(1)])
    scan_group(1, [functools.partial(c_proj, d, kb) for d in range(2) for kb in bundles(0)])
    for d, out_ref in enumerate((yf_ref, yb_ref)):
        for kb in bundles(1):
            c_proj(d, kb)
        for b in range(nb):
            for s in range(in_slabs):
                out_ref[b, :, s * LANES:(s + 1) * LANES] = y_ref[d, s, pl.ds(b, t, stride=nb), :]


def s5_scan(proj, bmat, cmat, atab, n_ctx):
    bsz, n, _ = proj.shape
    assert bsz * S5_HALVES == SUBLANES, "the scan layout puts (sample, state half) on the 8 sublanes"
    t = S5_TC
    nc = n // t
    c0 = n_ctx // t
    w = MIX_W
    rows = t * bsz
    bwd = lambda j: _rev_chunk(j, c0, nc)
    out = jax.ShapeDtypeStruct((bsz, n, w), F32)
    const = lambda a: pl.BlockSpec(a.shape, lambda j: (0,) * a.ndim)
    return pl.pallas_call(
        _s5_kernel, grid=(nc,),
        in_specs=[pl.BlockSpec((bsz, t, w), lambda j: (0, j, COL_US // w)),
                  pl.BlockSpec((bsz, t, w), lambda j: (0, bwd(j), COL_US // w)),
                  const(bmat), const(cmat), const(atab)],
        out_specs=[pl.BlockSpec((bsz, t, w), lambda j: (0, j, 0)),
                   pl.BlockSpec((bsz, t, w), lambda j: (0, bwd(j), 0))],
        out_shape=[out, out],
        scratch_shapes=[pltpu.VMEM((2, w // LANES, rows, LANES), F32),
                        pltpu.VMEM((2, S5_SLABS, rows * S5_HALVES, LANES), F32),
                        pltpu.VMEM((2, S5_SLABS, rows * S5_HALVES, LANES), F32),
                        pltpu.VMEM((2, w // LANES, rows, LANES), F32),
                        pltpu.VMEM((2, 2, S5_SLABS, SUBLANES, LANES), F32)],
        compiler_params=_params(("arbitrary",), 56),
        name="s5_scan",
    )(proj, proj, bmat, cmat, atab)


def s5_operands(lam_re, lam_im, log_dt, b_re, b_im, c_re, c_im):
    lam_re, lam_im = lam_re.astype(F32), lam_im.astype(F32)
    dt = jnp.exp(log_dt.astype(F32))[..., None]
    mag = jnp.exp(lam_re * dt)
    a_re, a_im = mag * jnp.cos(lam_im * dt), mag * jnp.sin(lam_im * dt)
    nr, ni = a_re - 1.0, a_im
    den = lam_re * lam_re + lam_im * lam_im
    fr = (nr * lam_re + ni * lam_im) / den
    fi = (ni * lam_re - nr * lam_im) / den
    bb_re = fr[..., None] * b_re - fi[..., None] * b_im
    bb_im = fr[..., None] * b_im + fi[..., None] * b_re
    eye = jnp.eye(S5_BUNDLE, dtype=F32)

    def blockdiag_b(bb):
        x = bb.reshape(2, S5_NB, S5_BUNDLE, S5_STATE, S5_GROUP)
        return jnp.einsum('dkgpi,gh->dkgihp', x, eye).reshape(
            2, S5_NB, S5_BUNDLE * S5_GROUP, S5_BUNDLE * S5_STATE)

    bmat = jnp.concatenate([blockdiag_b(bb_re), blockdiag_b(bb_im)], -1).astype(BF16)

    def blockdiag_c(c):
        x = c.astype(F32).reshape(S5_NB, S5_BUNDLE, S5_GROUP, S5_STATE)
        return jnp.einsum('kgip,gh->kgphi', x, eye).reshape(
            S5_NB, S5_BUNDLE * S5_STATE, S5_BUNDLE * S5_GROUP)

    cmat = jnp.stack([blockdiag_c(c_re), -blockdiag_c(c_im)], 1).astype(BF16)

    def slab_table(a):
        a4 = a.reshape(2, S5_HALVES, S5_SLABS, LANES)
        return jnp.take(a4, jnp.arange(SUBLANES) % S5_HALVES, axis=1).transpose(0, 2, 1, 3)

    atab = jnp.stack([slab_table(a_re), slab_table(a_im)], 1)
    return bmat, cmat, atab


def _glu_kernel(yf_ref, yb_ref, u_ref, d_ref, w_ref, b_ref, o_ref):
    y = d_ref[...] * u_ref[0] + yf_ref[0] + yb_ref[0]
    z = jax.nn.gelu(y)
    o_ref[0] = (z * jax.nn.sigmoid(_dot(z.astype(BF16), w_ref[...].astype(BF16)) + b_ref[...])).astype(o_ref.dtype)


def s5_glu(yf, yb, proj, d_skip, glu_w, glu_b, layer):
    bsz, n, w = yf.shape
    tm = ROW_TILE
    tile = pl.BlockSpec((1, tm, w), lambda b, t: (b, t, 0))
    vec = pl.BlockSpec((None, 1, w), lambda b, t: (layer, 0, 0))
    return pl.pallas_call(
        _glu_kernel, grid=(bsz, n // tm),
        in_specs=[tile, tile, pl.BlockSpec((1, tm, w), lambda b, t: (b, t, COL_US // w)),
                  vec, pl.BlockSpec((None, w, w), lambda b, t: (layer, 0, 0)), vec],
        out_specs=tile, out_shape=jax.ShapeDtypeStruct((bsz, n, w), BF16),
        compiler_params=_params(("arbitrary", "arbitrary"), 40),
        name="s5_glu",
    )(yf, yb, proj, d_skip.reshape(-1, 1, w), glu_w, glu_b.reshape(-1, 1, w))


def _da_prep_kernel(x_ref, g_ref, cos_ref, sin_ref, o_ref):
    cos = cos_ref[...]
    sin = sin_ref[...]
    lane = lax.broadcasted_iota(jnp.int32, cos.shape, 1)
    first = lane < DA_DQK
    quarter = DA_DQK // 4
    up = (lane & quarter) == 0
    scale = jnp.where(pl.program_id(2) == 0, DA_DQK ** -0.5 * math.log2(math.e), 1.0)
    for hd in range(DA_HEADS):
        sl = slice(hd * LANES, (hd + 1) * LANES)
        x = x_ref[0, :, sl]
        x2 = x * x
        ms0 = jnp.sum(jnp.where(first, x2, 0.0), axis=-1, keepdims=True)
        ms1 = jnp.sum(jnp.where(first, 0.0, x2), axis=-1, keepdims=True)
        ms = jnp.where(first, ms0, ms1) * (1.0 / DA_DQK)
        y = x * lax.rsqrt(ms + EPS) * g_ref[:, sl]
        partner = jnp.where(up, pltpu.roll(y, LANES - quarter, 1), pltpu.roll(y, quarter, 1))
        o_ref[0, :, sl] = ((y * cos + partner * sin) * scale).astype(o_ref.dtype)


def da_prep(proj, qn_g, kn_g, cos_t, sin_t, layer):
    bsz, n, _ = proj.shape
    tm = ROW_TILE
    w = DA_HEADS * LANES
    reps = w // DA_DQK
    gains = jnp.concatenate([jnp.tile(qn_g, (1, reps)), jnp.tile(kn_g, (1, reps))], -1)
    gains = gains.reshape(gains.shape[0], 1, 2 * w)
    return pl.pallas_call(
        _da_prep_kernel, grid=(bsz, n // tm, 2),
        in_specs=[pl.BlockSpec((1, tm, w), lambda b, t, j: (b, t, COL_QD // w + j)),
                  pl.BlockSpec((None, 1, w), lambda b, t, j: (layer, 0, j)),
                  pl.BlockSpec((tm, LANES), lambda b, t, j: (t, 0)),
                  pl.BlockSpec((tm, LANES), lambda b, t, j: (t, 0))],
        out_specs=pl.BlockSpec((1, tm, w), lambda b, t, j: (b, t, j)),
        out_shape=jax.ShapeDtypeStruct((bsz, n, 2 * w), BF16),
        compiler_params=_params(("arbitrary", "arbitrary", "arbitrary"), 40),
        name="da_prep",
    )(proj, gains, cos_t, sin_t)


def rope_tables(n, n_ctx):
    lat = jnp.arange(n) - n_ctx
    rows = (lat // GRID_W).astype(F32)
    cols = (lat % GRID_W).astype(F32)
    lane = jnp.arange(LANES)
    in_comp = lane % DA_DQK
    quarter = DA_DQK // 4
    freqs = ROPE_BASE ** (-(in_comp % quarter).astype(F32) / quarter)
    pos = jnp.where((in_comp < DA_DQK // 2)[None, :], rows[:, None], cols[:, None])
    ang = pos * freqs[None, :]
    sign = jnp.where((in_comp % (2 * quarter)) < quarter, -1.0, 1.0)
    is_lat = (lat >= 0)[:, None]
    cos_t = jnp.where(is_lat, jnp.cos(ang), 1.0)
    sin_t = jnp.where(is_lat, jnp.sin(ang) * sign[None, :], 0.0)
    return cos_t.astype(F32), sin_t.astype(F32)


DA_HEADS_PER_STEP = 2


def _diff_attn_kernel(scal_ref, q_ref, k_ref, v_ref, g_ref, o_ref, *, n_ctx, ctx_tiles):
    lam = scal_ref[0]
    out_scale = scal_ref[1]
    lane = lax.broadcasted_iota(jnp.int32, (q_ref.shape[1], LANES), 1)
    first = lane < DA_DQK

    def exp_rows(s):
        e = jnp.exp2(s - jnp.max(s, axis=-1, keepdims=True))
        return e, jnp.sum(e, axis=-1, keepdims=True)

    def attend(nk):
        for hh in range(DA_HEADS_PER_STEP):
            sl = slice(hh * LANES, (hh + 1) * LANES)
            q = q_ref[0, :, sl]
            zero = jnp.zeros_like(q)
            k = k_ref[0, :nk, sl]
            e0, l0 = exp_rows(_dot_nt(jnp.where(first, q, zero), k))
            e1, l1 = exp_rows(_dot_nt(jnp.where(first, zero, q), k))
            a = e0 - (lam * l0 * (1.0 / l1)) * e1
            o = _dot(a.astype(BF16), v_ref[0, :nk, sl].astype(BF16)) * (1.0 / l0)
            o = o * lax.rsqrt(jnp.mean(o * o, axis=-1, keepdims=True) + EPS) * g_ref[...]
            o_ref[0, :, sl] = (o * out_scale).astype(o_ref.dtype)

    @pl.when(pl.program_id(2) < ctx_tiles)
    def _():
        attend(n_ctx)

    @pl.when(pl.program_id(2) >= ctx_tiles)
    def _():
        attend(k_ref.shape[1])


def diff_attn(qk_hat, proj, scal, sub_g, layer, n_ctx):
    bsz, n, _ = qk_hat.shape
    tq = SEQ_CHUNK
    w = DA_HEADS_PER_STEP * LANES
    return pl.pallas_call(
        functools.partial(_diff_attn_kernel, n_ctx=n_ctx, ctx_tiles=n_ctx // tq),
        grid=(bsz, DA_HEADS // DA_HEADS_PER_STEP, n // tq),
        in_specs=[pl.BlockSpec(memory_space=pltpu.SMEM),
                  pl.BlockSpec((1, tq, w), lambda b, h, t: (b, t, h)),
                  pl.BlockSpec((1, n, w), lambda b, h, t: (b, 0, DA_HEADS // DA_HEADS_PER_STEP + h)),
                  pl.BlockSpec((1, n, w), lambda b, h, t: (b, 0, COL_VD // w + h)),
                  pl.BlockSpec((None, 1, LANES), lambda b, h, t: (layer, 0, 0))],
        out_specs=pl.BlockSpec((1, tq, w), lambda b, h, t: (b, t, h)),
        out_shape=jax.ShapeDtypeStruct((bsz, n, DA_HEADS * DA_DV), BF16),
        compiler_params=_params(("arbitrary", "arbitrary", "arbitrary"), 48),
        name="diff_attn",
    )(scal, qk_hat, qk_hat, proj, sub_g.reshape(sub_g.shape[0], 1, DA_DV))


def _merge_kernel(ya_ref, yb_ref, yc_ref, p_ref, ga_ref, gb_ref, gc_ref, o_ref):
    acc = _dot(ya_ref[0], p_ref[0].astype(BF16)) * jax.nn.sigmoid(ga_ref[0])
    acc += _dot(yb_ref[0], p_ref[1].astype(BF16)) * jax.nn.sigmoid(gb_ref[0])
    acc += _dot(yc_ref[0], p_ref[2].astype(BF16)) * jax.nn.sigmoid(gc_ref[0])
    o_ref[0] = acc.astype(o_ref.dtype)


def branch_merge(ya, yb, yc, proj, branch_proj, layer):
    bsz, n, w = ya.shape
    d = branch_proj.shape[-1]
    tm, tn = 768, 512
    y_tile = pl.BlockSpec((1, tm, w), lambda j, b, t: (b, t, 0))
    gate = lambda r: pl.BlockSpec((1, tm, tn), lambda j, b, t: (b, t, (COL_GB + r * d) // tn + j))
    return pl.pallas_call(
        _merge_kernel, grid=(d // tn, bsz, n // tm),
        in_specs=[y_tile, y_tile, y_tile,
                  pl.BlockSpec((None, N_BRANCH, w, tn), lambda j, b, t: (layer, 0, 0, j)),
                  gate(0), gate(1), gate(2)],
        out_specs=pl.BlockSpec((1, tm, tn), lambda j, b, t: (b, t, j)),
        out_shape=jax.ShapeDtypeStruct((bsz, n, d), BF16),
        compiler_params=_params(("arbitrary", "arbitrary", "arbitrary"), 48),
        name="branch_merge",
    )(ya, yb, yc, branch_proj, proj, proj, proj)


INFO_E, INFO_POS, INFO_W = 0, 2, 4


def _router_kernel(h_ref, w_ref, b_ref, info_ref, cnt_ref):
    tm = h_ref.shape[1]

    @pl.when((pl.program_id(0) == 0) & (pl.program_id(1) == 0))
    def _():
        cnt_ref[...] = jnp.zeros_like(cnt_ref)

    h_parts = _split3(h_ref[0])
    w_parts = _split3(w_ref[...])
    logits = (_dot_nt(w_parts[0], h_parts[0]) + _dot_nt(w_parts[0], h_parts[1])
              + _dot_nt(w_parts[1], h_parts[0]))
    score = jax.nn.sigmoid(logits)
    sel = score + b_ref[...]
    rows = [sel[e:e + 1, :] for e in range(N_EXPERTS)]
    gscores = []
    for g in range(N_EXPERT_GROUPS):
        a, b, c, d = rows[g * EXPERTS_PER_GROUP:(g + 1) * EXPERTS_PER_GROUP]
        hi1, lo1 = jnp.maximum(a, b), jnp.minimum(a, b)
        hi2, lo2 = jnp.maximum(c, d), jnp.minimum(c, d)
        gscores.append(jnp.maximum(hi1, hi2) + jnp.maximum(jnp.minimum(hi1, hi2), jnp.maximum(lo1, lo2)))
    gmax = functools.reduce(jnp.maximum, gscores)
    taken = jnp.zeros_like(gmax, dtype=jnp.bool_)
    picked = []
    for g in range(N_EXPERT_GROUPS):
        is_g = (gscores[g] == gmax) & jnp.logical_not(taken)
        picked.append(is_g)
        taken = taken | is_g
    chosen = []
    for e in range(N_EXPERTS):
        g = e // EXPERTS_PER_GROUP
        rank = jnp.zeros_like(gmax)
        for o in range(g * EXPERTS_PER_GROUP, (g + 1) * EXPERTS_PER_GROUP):
            if o == e:
                continue
            ahead = (rows[o] > rows[e]) | ((rows[o] == rows[e]) & (o < e))
            rank = rank + ahead.astype(F32)
        chosen.append(picked[g] & (rank < float(TOP_K)))
    onehot = jnp.concatenate([c.astype(F32) for c in chosen], axis=0)
    total = jnp.sum(onehot * score, axis=0, keepdims=True)
    gates = onehot * score * (1.0 / total)

    r_i = lax.broadcasted_iota(jnp.int32, (tm, tm), 0)
    c_i = lax.broadcasted_iota(jnp.int32, (tm, tm), 1)
    incl = _dot(onehot.astype(BF16), (r_i <= c_i).astype(BF16))
    pos = incl - onehot + cnt_ref[:, :1]
    cnt_ref[...] = cnt_ref[...] + incl[:, tm - 1:tm]

    seen = jnp.zeros_like(gmax)
    rec = [jnp.zeros_like(gmax) for _ in range(3 * TOP_K)]
    for e in range(N_EXPERTS):
        oh = onehot[e:e + 1, :]
        for k, is_k in enumerate((oh * (1.0 - seen), oh * seen)):
            rec[INFO_E + k] += is_k * float(e)
            rec[INFO_POS + k] += is_k * pos[e:e + 1, :]
            rec[INFO_W + k] += is_k * gates[e:e + 1, :]
        seen = seen + oh
    pad = jnp.zeros((LANES - len(rec), tm), F32)
    info_ref[0] = jnp.concatenate(rec + [pad], axis=0).T


def moe_router(h2, router_w, router_b):
    bsz, n, d = h2.shape
    tm = SEQ_CHUNK
    return pl.pallas_call(
        _router_kernel, grid=(bsz, n // tm),
        in_specs=[pl.BlockSpec((1, tm, d), lambda b, t: (b, t, 0)),
                  pl.BlockSpec((N_EXPERTS, d), lambda b, t: (0, 0)),
                  pl.BlockSpec((N_EXPERTS, 1), lambda b, t: (0, 0))],
        out_specs=[pl.BlockSpec((1, tm, LANES), lambda b, t: (b, t, 0)),
                   pl.BlockSpec((N_EXPERTS, LANES), lambda b, t: (0, 0))],
        out_shape=[jax.ShapeDtypeStruct((bsz, n, LANES), F32),
                   jax.ShapeDtypeStruct((N_EXPERTS, LANES), F32)],
        compiler_params=_params(("arbitrary", "arbitrary"), 40),
        name="moe_router",
    )(h2, router_w.T, router_b.reshape(N_EXPERTS, 1))


def moe_plan(info, counts, n_tiles):
    m = info.shape[0] * info.shape[1]
    rec = info.reshape(m, LANES)
    cnt = counts[:, 0].astype(jnp.int32)
    padded = (cnt + MOE_TILE - 1) // MOE_TILE * MOE_TILE
    ends = jnp.cumsum(padded)
    starts = ends - padded
    experts = rec[:, INFO_E:INFO_E + TOP_K].astype(jnp.int32)
    dest = starts[experts] + rec[:, INFO_POS:INFO_POS + TOP_K].astype(jnp.int32)
    tile_start = jnp.arange(n_tiles, dtype=jnp.int32) * MOE_TILE
    tile_expert = jnp.minimum(jnp.sum(ends[None, :] <= tile_start[:, None], axis=1), N_EXPERTS - 1).astype(jnp.int32)
    n_used = (ends[-1] // MOE_TILE).astype(jnp.int32).reshape(1)
    return dest.reshape(-1), tile_expert, n_used


def _row_copy(src_ref, src_row, dst_ref, dst_row, sem):
    return pltpu.make_async_copy(src_ref.at[pl.ds(src_row, 1), :], dst_ref.at[pl.ds(dst_row, 1), :], sem)


def _dispatch_kernel(dest_ref, h_ref, xs_in_ref, xs_ref, sem):
    del xs_in_ref
    tm = h_ref.shape[0]
    base = pl.program_id(0) * tm

    def start(r, carry):
        for k in range(TOP_K):
            _row_copy(h_ref, r, xs_ref, dest_ref[(base + r) * TOP_K + k], sem).start()
        return carry

    def wait(r, carry):
        for k in range(TOP_K):
            _row_copy(h_ref, 0, xs_ref, 0, sem).wait()
        return carry

    lax.fori_loop(0, tm, start, 0, unroll=DMA_UNROLL)
    lax.fori_loop(0, tm, wait, 0, unroll=DMA_UNROLL)


def moe_dispatch(h2, dest, xs_init):
    m, d = h2.shape
    n_rows = xs_init.shape[0]
    tm = SEQ_CHUNK
    grid_spec = pltpu.PrefetchScalarGridSpec(
        num_scalar_prefetch=1, grid=(m // tm,),
        in_specs=[pl.BlockSpec((tm, d), lambda i, dest: (i, 0)),
                  pl.BlockSpec(memory_space=pl.ANY)],
        out_specs=pl.BlockSpec(memory_space=pl.ANY),
        scratch_shapes=[pltpu.SemaphoreType.DMA])
    return pl.pallas_call(
        _dispatch_kernel, grid_spec=grid_spec,
        out_shape=jax.ShapeDtypeStruct((n_rows, d), h2.dtype),
        input_output_aliases={2: 0},
        compiler_params=_params(("arbitrary",), 40),
        name="moe_dispatch",
    )(dest, h2, xs_init)


def _expert_kernel(te_ref, nu_ref, x_ref, wg_ref, wu_ref, wd_ref, o_ref, wg_scr, wu_scr, wd_scr):
    i = pl.program_id(0)
    used = i < nu_ref[0]

    @pl.when(used & ((i == 0) | (te_ref[i] != te_ref[jnp.maximum(i - 1, 0)])))
    def _():
        wg_scr[...] = wg_ref[...].astype(BF16)
        wu_scr[...] = wu_ref[...].astype(BF16)
        wd_scr[...] = wd_ref[...].astype(BF16)

    @pl.when(used)
    def _():
        x = x_ref[...].astype(BF16)
        a = _dot(x, wg_scr[...])
        act = (a * jax.nn.sigmoid(a)) * _dot(x, wu_scr[...])
        o_ref[...] = _dot(act.astype(BF16), wd_scr[...])

    @pl.when(jnp.logical_not(used))
    def _():
        o_ref[...] = jnp.zeros_like(o_ref)


def moe_experts(xs, tile_expert, n_used, w_gate, w_up, w_down, layer):
    n_rows, d = xs.shape
    f = w_gate.shape[-1]
    w_spec = lambda r, c: pl.BlockSpec((None, None, r, c), lambda i, te, nu: (layer, te[i], 0, 0))
    grid_spec = pltpu.PrefetchScalarGridSpec(
        num_scalar_prefetch=2, grid=(n_rows // MOE_TILE,),
        in_specs=[pl.BlockSpec((MOE_TILE, d), lambda i, te, nu: (i, 0)),
                  w_spec(d, f), w_spec(d, f), w_spec(f, d)],
        out_specs=pl.BlockSpec((MOE_TILE, d), lambda i, te, nu: (i, 0)),
        scratch_shapes=[pltpu.VMEM((d, f), BF16), pltpu.VMEM((d, f), BF16), pltpu.VMEM((f, d), BF16)])
    return pl.pallas_call(
        _expert_kernel, grid_spec=grid_spec,
        out_shape=jax.ShapeDtypeStruct((n_rows, d), F32),
        compiler_params=_params(("arbitrary",), 56),
        name="moe_experts",
    )(tile_expert, n_used, xs, w_gate, w_up, w_down)


def _combine_kernel(dest_ref, ys_ref, info_ref, z_ref, gate_ref, o_ref, buf_ref, sem):
    tm = z_ref.shape[1]
    base = (pl.program_id(0) * pl.num_programs(1) + pl.program_id(1)) * tm

    def start(r, carry):
        for k in range(TOP_K):
            _row_copy(ys_ref, dest_ref[(base + r) * TOP_K + k], buf_ref.at[k], r, sem).start()
        return carry

    def wait(r, carry):
        for k in range(TOP_K):
            _row_copy(ys_ref, 0, buf_ref.at[k], 0, sem).wait()
        return carry

    lax.fori_loop(0, tm, start, 0, unroll=DMA_UNROLL)
    lax.fori_loop(0, tm, wait, 0, unroll=DMA_UNROLL)
    info = info_ref[0]
    mix = info[:, INFO_W:INFO_W + 1] * buf_ref[0] + info[:, INFO_W + 1:INFO_W + 2] * buf_ref[1]
    o_ref[0] = z_ref[0] + gate_ref[...] * mix


def moe_combine(ys, dest, info, z, mods, layer, gate_chunk, n_ctx):
    bsz, n, d = z.shape
    tm = SEQ_CHUNK
    ctx_tiles = n_ctx // tm
    tile = pl.BlockSpec((1, tm, d), lambda b, t, dest: (b, t, 0))
    grid_spec = pltpu.PrefetchScalarGridSpec(
        num_scalar_prefetch=1, grid=(bsz, n // tm),
        in_specs=[pl.BlockSpec(memory_space=pl.ANY),
                  pl.BlockSpec((1, tm, LANES), lambda b, t, dest: (b, t, 0)),
                  tile,
                  pl.BlockSpec((None, None, 1, d),
                               lambda b, t, dest: (layer, jnp.where(t < ctx_tiles, bsz, b), 0, gate_chunk))],
        out_specs=tile,
        scratch_shapes=[pltpu.VMEM((TOP_K, tm, d), F32), pltpu.SemaphoreType.DMA])
    return pl.pallas_call(
        _combine_kernel, grid_spec=grid_spec,
        out_shape=jax.ShapeDtypeStruct(z.shape, F32),
        compiler_params=_params(("arbitrary", "arbitrary"), 40),
        name="moe_combine",
    )(dest, ys, info, z, mods)


def kernel(x, c, ctx, c_ctx, norm1_g, norm2_g, ada_w, ada_b, w_in, b_in, ml_conv_w, ml_conv_b, ml_norm_g, s5_lam_re, s5_lam_im, s5_log_dt, s5_b_re, s5_b_im, s5_c_re, s5_c_im, s5_d, s5_glu_w, s5_glu_b, da_q_norm_g, da_k_norm_g, da_lam_q1, da_lam_k1, da_lam_q2, da_lam_k2, da_sub_norm_g, branch_proj, w_out, router_w, router_b, exp_w_gate, exp_w_up, exp_w_down):
    bsz, n_lat, d = x.shape
    n_ctx = ctx.shape[1]
    n = n_ctx + n_lat
    depth = w_in.shape[0]
    assert n_ctx % SEQ_CHUNK == 0 and n_lat % SEQ_CHUNK == 0 and d == D_MODEL

    rows = -(-(bsz + 1) // SUBLANES) * SUBLANES
    cvec = jnp.zeros((rows, d), F32).at[:bsz].set(c).at[bsz].set(c_ctx)
    mods = ada_mod(cvec, ada_w, ada_b).reshape(depth, rows, 1, 6 * d)

    w_in_t = jnp.swapaxes(w_in, 1, 2)
    b_in3 = b_in[:, None, :]
    b_ml = b_in3[..., :GATE_OFF]
    b_rest = b_in3[..., REST_OFF:]
    m_tok = bsz * n
    moe_tiles = m_tok * TOP_K // MOE_TILE + N_EXPERTS
    xs = jnp.zeros((moe_tiles * MOE_TILE, d), F32)
    cos_t, sin_t = rope_tables(n, n_ctx)
    z = jnp.concatenate([ctx, x], 1)
    for i in range(depth):
        lam_init = 0.8 - 0.6 * math.exp(-0.3 * i)
        (h,) = norm_mod(z, norm1_g, mods, i, 0, 1, n_ctx, (BF16,))
        proj_ml = in_proj(h, w_in_t, b_ml, i, 0)
        proj = in_proj(h, w_in_t, b_rest, i, REST_OFF)
        gates = gate_proj(h, w_in_t, b_in3, i, GATE_OFF)
        gates_t = jnp.swapaxes(gates[..., :N_GATES], 1, 2)

        qk, qkt = conv_silu(proj_ml, ml_conv_w, ml_conv_b, i, n_ctx)
        hf, hb = mlstm(qk, qkt, proj_ml, gates, gates_t, n_ctx)
        ya = mlstm_out(hf, hb, proj_ml, ml_norm_g, i)

        bmat, cmat, atab = s5_operands(s5_lam_re[i], s5_lam_im[i], s5_log_dt[i], s5_b_re[i], s5_b_im[i],
                                       s5_c_re[i], s5_c_im[i])
        yf, ybk = s5_scan(proj, bmat, cmat, atab, n_ctx)
        yb = s5_glu(yf, ybk, proj, s5_d, s5_glu_w, s5_glu_b, i)

        qk_hat = da_prep(proj, da_q_norm_g, da_k_norm_g, cos_t, sin_t, i)
        lam = (jnp.exp(jnp.sum(da_lam_q1[i].astype(F32) * da_lam_k1[i].astype(F32)))
               - jnp.exp(jnp.sum(da_lam_q2[i].astype(F32) * da_lam_k2[i].astype(F32))) + lam_init)
        scal = jnp.stack([lam, jnp.asarray(1.0 - lam_init, F32)]).astype(F32)
        yc = diff_attn(qk_hat, proj, scal, da_sub_norm_g, i, n_ctx)

        merged = branch_merge(ya, yb, yc, proj, branch_proj, i)
        z = matmul_resid(merged, w_out, z, mods, i, 2, n_ctx)

        (h2,) = norm_mod(z, norm2_g, mods, i, 3, 4, n_ctx, (F32,))
        info, counts = moe_router(h2, router_w, router_b)
        dest, tile_expert, n_used = moe_plan(info, counts, moe_tiles)
        xs = moe_dispatch(h2.reshape(m_tok, d), dest, xs)
        ys = moe_experts(xs, tile_expert, n_used, exp_w_gate, exp_w_up, exp_w_down, i)
        z = moe_combine(ys, dest, info, z, mods, i, 5, n_ctx)
    return z[:, n_ctx:]
```

```python
import functools
import math

import jax
import jax.numpy as jnp
from jax import lax
from jax.experimental import pallas as pl
from jax.experimental.pallas import tpu as pltpu

F32 = jnp.float32
BF16 = jnp.bfloat16

D_MODEL = 2048
GRID_W = 64
MIX_W = 1024
N_BRANCH = 3
ML_HEADS = 4
ML_DH = MIX_W // ML_HEADS
S5_GROUP = 16
S5_GROUPS = MIX_W // S5_GROUP
S5_STATE = 64
S5_BUNDLE = 16
S5_NB = S5_GROUPS // S5_BUNDLE
S5_NS = S5_GROUPS * S5_STATE
DA_HEADS = 8
DA_DQK = 64
DA_DV = 2 * DA_DQK
ROPE_BASE = 10000.0
N_EXPERTS = 16
N_EXPERT_GROUPS = 4
EXPERTS_PER_GROUP = N_EXPERTS // N_EXPERT_GROUPS
TOP_K = 2
D_FF_EXPERT = 512
EPS = 1e-6

COL_QA, COL_KA, COL_VA, COL_OA = 0, 1024, 2048, 3072
COL_US, COL_QD, COL_KD, COL_VD, COL_GB = 0, 1024, 2048, 3072, 4096
GATE_OFF = 4 * MIX_W
N_GATES = 4 * ML_HEADS
REST_OFF = GATE_OFF + N_GATES
REST_COLS = 4 * MIX_W + N_BRANCH * D_MODEL

SEQ_CHUNK = 256
ROW_TILE = 768
LANES = 128
SUBLANES = 8
MIB = 1024 * 1024

S5_TC = 64
S5_HALVES = 2
S5_SLABS = S5_NS // S5_HALVES // LANES
S5_GRP = 8
MOE_TILE = 512
DMA_UNROLL = 8


def _params(sem, vmem_mib):
    return pltpu.CompilerParams(dimension_semantics=sem, vmem_limit_bytes=vmem_mib * MIB)


def _dot(a, b):
    return jnp.dot(a, b, preferred_element_type=F32)


def _dot_nt(a, b):
    return lax.dot_general(a, b, (((1,), (1,)), ((), ())), preferred_element_type=F32)


def _split3(x):
    hi = x.astype(BF16)
    r = x - hi.astype(F32)
    mid = r.astype(BF16)
    lo = (r - mid.astype(F32)).astype(BF16)
    return hi, mid, lo


def _rev_chunk(j, n_ctx_chunks, n_chunks):
    return jnp.where(j < n_ctx_chunks, n_ctx_chunks - 1 - j, n_chunks - 1 - (j - n_ctx_chunks))


def _ada_kernel(c_ref, w_ref, b_ref, o_ref):
    c = c_ref[...]
    s = (c * jax.nn.sigmoid(c)).astype(BF16)
    o_ref[0] = _dot(s, w_ref[0].astype(BF16)) + b_ref[0]


def ada_mod(cvec, ada_w, ada_b):
    depth, d, n6 = ada_w.shape
    rows = cvec.shape[0]
    tn = 1024
    return pl.pallas_call(
        _ada_kernel,
        grid=(depth, n6 // tn),
        in_specs=[pl.BlockSpec((rows, d), lambda l, j: (0, 0)),
                  pl.BlockSpec((1, d, tn), lambda l, j: (l, 0, j)),
                  pl.BlockSpec((1, 1, tn), lambda l, j: (l, 0, j))],
        out_specs=pl.BlockSpec((1, rows, tn), lambda l, j: (l, 0, j)),
        out_shape=jax.ShapeDtypeStruct((depth, rows, n6), F32),
        compiler_params=_params(("arbitrary", "arbitrary"), 40),
        name="ada_mod",
    )(cvec, ada_w, ada_b.reshape(depth, 1, n6))


def _mod_spec(layer, chunk, batch, ctx_tiles):
    return pl.BlockSpec((None, None, 1, D_MODEL),
                        lambda b, t: (layer, jnp.where(t < ctx_tiles, batch, b), 0, chunk))


def _norm_mod_kernel(z_ref, g_ref, shift_ref, scale_ref, *outs):
    z = z_ref[0]
    y = z * lax.rsqrt(jnp.mean(z * z, axis=-1, keepdims=True) + EPS) * g_ref[...]
    h = y * (1.0 + scale_ref[...]) + shift_ref[...]
    for o in outs:
        o[0] = h.astype(o.dtype)


def norm_mod(z, norm_g, mods, layer, shift_chunk, scale_chunk, n_ctx, out_dtypes):
    bsz, n, d = z.shape
    tm = SEQ_CHUNK
    ctx_tiles = n_ctx // tm
    tile = pl.BlockSpec((1, tm, d), lambda b, t: (b, t, 0))
    return pl.pallas_call(
        _norm_mod_kernel, grid=(bsz, n // tm),
        in_specs=[tile, pl.BlockSpec((None, 1, d), lambda b, t: (layer, 0, 0)),
                  _mod_spec(layer, shift_chunk, bsz, ctx_tiles), _mod_spec(layer, scale_chunk, bsz, ctx_tiles)],
        out_specs=[tile] * len(out_dtypes),
        out_shape=[jax.ShapeDtypeStruct(z.shape, dt) for dt in out_dtypes],
        compiler_params=_params(("arbitrary", "arbitrary"), 40),
        name="norm_mod",
    )(z, norm_g.reshape(norm_g.shape[0], 1, d), mods, mods)


def _in_proj_kernel(x_ref, wt_ref, b_ref, o_ref, w_scr):
    @pl.when((pl.program_id(1) == 0) & (pl.program_id(2) == 0))
    def _():
        rows = 256
        for r in range(0, wt_ref.shape[0], rows):
            w_scr[:, r:r + rows] = wt_ref[r:r + rows, :].T.astype(BF16)

    o_ref[0] = (_dot(x_ref[0], w_scr[...]) + b_ref[...]).astype(o_ref.dtype)


def in_proj(x, w_t, bias, layer, row0, out_dtype=F32):
    bsz, n, k = x.shape
    n_cols = bias.shape[-1]
    tm, tn = n // 2, 1024
    return pl.pallas_call(
        _in_proj_kernel, grid=(n_cols // tn, bsz, n // tm),
        in_specs=[pl.BlockSpec((1, tm, k), lambda j, b, t: (b, t, 0)),
                  pl.BlockSpec((None, pl.Element(tn), pl.Element(k)), lambda j, b, t: (layer, pl.multiple_of(row0 + j * tn, SUBLANES), 0)),
                  pl.BlockSpec((None, 1, tn), lambda j, b, t: (layer, 0, j))],
        out_specs=pl.BlockSpec((1, tm, tn), lambda j, b, t: (b, t, j)),
        out_shape=jax.ShapeDtypeStruct((bsz, n, n_cols), out_dtype),
        scratch_shapes=[pltpu.VMEM((k, tn), BF16)],
        compiler_params=_params(("arbitrary", "arbitrary", "arbitrary"), 56),
        name="in_proj",
    )(x, w_t, bias)


def _gates_kernel(x_ref, wt_ref, b_ref, o_ref):
    o_ref[0] = _dot_nt(x_ref[0], wt_ref[...].astype(BF16)) + b_ref[...]


def gate_proj(x, w_t, bias, layer, row0):
    bsz, n, k = x.shape
    tm = n // 2
    rb = row0 // LANES
    return pl.pallas_call(
        _gates_kernel, grid=(bsz, n // tm),
        in_specs=[pl.BlockSpec((1, tm, k), lambda b, t: (b, t, 0)),
                  pl.BlockSpec((None, LANES, k), lambda b, t: (layer, rb, 0)),
                  pl.BlockSpec((None, 1, LANES), lambda b, t: (layer, 0, rb))],
        out_specs=pl.BlockSpec((1, tm, LANES), lambda b, t: (b, t, 0)),
        out_shape=jax.ShapeDtypeStruct((bsz, n, LANES), F32),
        compiler_params=_params(("arbitrary", "arbitrary"), 40),
        name="gate_proj",
    )(x, w_t, bias)


def _mm_resid_kernel(x_ref, w_ref, z_ref, gctx_ref, glat_ref, o_ref, *, n_ctx):
    acc = _dot(x_ref[0].astype(BF16), w_ref[...].astype(BF16))
    tm = acc.shape[0]
    row = pl.program_id(2) * tm + lax.broadcasted_iota(jnp.int32, (tm, 1), 0)
    gate = jnp.where(row < n_ctx, gctx_ref[...], glat_ref[...])
    o_ref[0] = z_ref[0] + gate * acc


def matmul_resid(x, w, z, mods, layer, gate_chunk, n_ctx):
    bsz, n, k = x.shape
    nc = w.shape[-1]
    tm, tn = 768, 512
    gate_spec = lambda ctx: pl.BlockSpec(
        (None, None, 1, tn),
        lambda j, b, t: (layer, bsz if ctx else b, 0, gate_chunk * (nc // tn) + j))
    return pl.pallas_call(
        functools.partial(_mm_resid_kernel, n_ctx=n_ctx), grid=(nc // tn, bsz, n // tm),
        in_specs=[pl.BlockSpec((1, tm, k), lambda j, b, t: (b, t, 0)),
                  pl.BlockSpec((None, k, tn), lambda j, b, t: (layer, 0, j)),
                  pl.BlockSpec((1, tm, tn), lambda j, b, t: (b, t, j)),
                  gate_spec(True), gate_spec(False)],
        out_specs=pl.BlockSpec((1, tm, tn), lambda j, b, t: (b, t, j)),
        out_shape=jax.ShapeDtypeStruct((bsz, n, nc), F32),
        compiler_params=_params(("arbitrary", "arbitrary", "arbitrary"), 48),
        name="matmul_resid",
    )(x, w, z, mods, mods)


def _conv_kernel(x_ref, w_ref, b_ref, o_ref, ot_ref, *, n_ctx):
    x = x_ref[0]
    n = x.shape[0]
    row = lax.broadcasted_iota(jnp.int32, x.shape, 0)
    prev = jnp.where((row == 0) | (row == n_ctx), 0.0, pltpu.roll(x, 1, 0))
    nxt = jnp.where((row == n_ctx - 1) | (row == n - 1), 0.0, pltpu.roll(x, n - 1, 0))
    w = w_ref[...]
    y = b_ref[...] + prev * w[0:1] + x * w[1:2] + nxt * w[2:3]
    y = y * jax.nn.sigmoid(y)
    k_tiles = pl.num_programs(1) // 2
    y = y * jnp.where(pl.program_id(1) >= k_tiles, ML_DH ** -0.5, 1.0)
    o_ref[0] = y.astype(o_ref.dtype)
    ot_ref[0] = y.T.astype(ot_ref.dtype)


def conv_silu(proj, conv_w, conv_b, layer, n_ctx):
    bsz, n, _ = proj.shape
    tc = 256
    c2 = 2 * MIX_W
    return pl.pallas_call(
        functools.partial(_conv_kernel, n_ctx=n_ctx),
        grid=(bsz, c2 // tc),
        in_specs=[pl.BlockSpec((1, n, tc), lambda b, c: (b, 0, c)),
                  pl.BlockSpec((None, conv_w.shape[1], tc), lambda b, c: (layer, 0, c)),
                  pl.BlockSpec((None, 1, tc), lambda b, c: (layer, 0, c))],
        out_specs=[pl.BlockSpec((1, n, tc), lambda b, c: (b, 0, c)),
                   pl.BlockSpec((1, tc, n), lambda b, c: (b, c, 0))],
        out_shape=[jax.ShapeDtypeStruct((bsz, n, c2), BF16),
                   jax.ShapeDtypeStruct((bsz, c2, n), BF16)],
        compiler_params=_params(("arbitrary", "arbitrary"), 48),
        name="conv_silu",
    )(proj, conv_w, conv_b.reshape(conv_b.shape[0], 1, c2))


def _log_sigmoid(x):
    return jnp.minimum(x, 0.0) - jnp.log(1.0 + jnp.exp(-jnp.abs(x)))


def _mlstm_kernel(qf_ref, kf_ref, ktf_ref, vf_ref, gf_ref, gtf_ref,
                  qb_ref, kb_ref, ktb_ref, vb_ref, gb_ref, gtb_ref,
                  hf_ref, hb_ref, c_ref, m_ref):
    t = qf_ref.shape[1]
    dh = ML_DH

    @pl.when(pl.program_id(1) == 0)
    def _():
        c_ref[...] = jnp.zeros_like(c_ref)
        m_ref[...] = jnp.zeros_like(m_ref)

    r_i = lax.broadcasted_iota(jnp.int32, (t, t), 0)
    c_i = lax.broadcasted_iota(jnp.int32, (t, t), 1)
    low = c_i <= r_i
    upp = c_i >= r_i
    low_b = low.astype(BF16)
    upp_b = upp.astype(BF16)
    ones_col = (lax.broadcasted_iota(jnp.int32, (t, LANES), 1) == 0).astype(F32)

    dirs = ((qf_ref, kf_ref, ktf_ref, vf_ref, gf_ref, gtf_ref, hf_ref, low, low_b, upp_b),
            (qb_ref, kb_ref, ktb_ref, vb_ref, gb_ref, gtb_ref, hb_ref, upp, upp_b, low_b))
    for d, (q_ref, k_ref, kt_ref, v_ref, g_ref, gt_ref, h_ref, mask, col_tri, row_tri) in enumerate(dirs):
        g = g_ref[0]
        gt = gt_ref[0]
        lf = _log_sigmoid(g)
        lft = _log_sigmoid(gt)
        b_cols = sum(_dot(col_tri, p) for p in _split3(lf))
        b_rows = sum(_dot(p, row_tri) for p in _split3(lft))
        last = t - 1 if d == 0 else 0
        for hd in range(ML_HEADS):
            ci = 2 * d * ML_HEADS + hd
            cf = ci + ML_HEADS
            idx = d * ML_HEADS + hd
            sl = slice(hd * dh, (hd + 1) * dh)
            q = q_ref[0, :, sl]
            k = k_ref[0, :, sl]
            kt = kt_ref[0, sl, :]
            v = v_ref[0, :, sl]
            i_col = g[:, ci:ci + 1]
            i_row = gt[ci:ci + 1, :]
            b_col = b_cols[:, cf:cf + 1]
            b_row = b_rows[cf:cf + 1, :]
            b_tot = b_col[last:last + 1, :]
            m_prev = m_ref[idx][:, :1]
            c_prev = c_ref[idx]

            log_inter = b_col + m_prev
            log_intra = jnp.where(mask, b_col - b_row + i_row, -jnp.inf)
            m_t = jnp.maximum(log_inter, jnp.max(log_intra, axis=-1, keepdims=True))
            w_inter = jnp.exp(log_inter - m_t)
            s = _dot_nt(q, k) * jnp.exp(log_intra - m_t)
            v_ext = jnp.concatenate([v, ones_col], axis=1)
            nd = w_inter * _dot(q, c_prev.astype(BF16)) + _dot(s.astype(BF16), v_ext.astype(BF16))
            den = nd[:, dh:dh + 1]
            h_ref[0, :, sl] = nd[:, :dh] * (1.0 / jnp.maximum(jnp.abs(den), jnp.exp(-m_t)))

            log_w = b_tot - b_col + i_col
            m_new = jnp.maximum(b_tot + m_prev, jnp.max(log_w, axis=0, keepdims=True))
            decay = jnp.exp(b_tot + m_prev - m_new)
            w_s = jnp.exp(log_w - m_new)
            c_ref[idx] = decay * c_prev + _dot(kt, (v_ext * w_s).astype(BF16))
            m_ref[idx] = jnp.broadcast_to(m_new, (1, LANES))


def mlstm(qk, qkt, proj, gates, gates_t, n_ctx):
    bsz, n, _ = qk.shape
    t = SEQ_CHUNK
    nc = n // t
    c0 = n_ctx // t
    w = MIX_W
    fwd = lambda j: j
    bwd = lambda j: _rev_chunk(j, c0, nc)

    def specs(ch):
        return [pl.BlockSpec((1, t, w), lambda b, j: (b, ch(j), 0)),
                pl.BlockSpec((1, t, w), lambda b, j: (b, ch(j), 1)),
                pl.BlockSpec((1, w, t), lambda b, j: (b, 1, ch(j))),
                pl.BlockSpec((1, t, w), lambda b, j: (b, ch(j), COL_VA // w)),
                pl.BlockSpec((1, t, LANES), lambda b, j: (b, ch(j), 0)),
                pl.BlockSpec((1, N_GATES, t), lambda b, j: (b, 0, ch(j)))]

    args = [qk, qk, qkt, proj, gates, gates_t]
    out = jax.ShapeDtypeStruct((bsz, n, w), F32)
    return pl.pallas_call(
        _mlstm_kernel, grid=(bsz, nc),
        in_specs=specs(fwd) + specs(bwd),
        out_specs=[pl.BlockSpec((1, t, w), lambda b, j: (b, fwd(j), 0)),
                   pl.BlockSpec((1, t, w), lambda b, j: (b, bwd(j), 0))],
        out_shape=[out, out],
        scratch_shapes=[pltpu.VMEM((2 * ML_HEADS, ML_DH, ML_DH + LANES), F32),
                        pltpu.VMEM((2 * ML_HEADS, 1, LANES), F32)],
        compiler_params=_params(("arbitrary", "arbitrary"), 48),
        name="mlstm",
    )(*(args + args))


def _ya_kernel(hf_ref, hb_ref, oa_ref, g_ref, o_ref):
    h = hf_ref[0] + hb_ref[0]
    parts = []
    for hd in range(ML_HEADS):
        x = h[:, hd * ML_DH:(hd + 1) * ML_DH]
        parts.append(x * lax.rsqrt(jnp.mean(x * x, axis=-1, keepdims=True) + EPS))
    y = jnp.concatenate(parts, axis=1) * g_ref[...]
    o_ref[0] = (jax.nn.sigmoid(oa_ref[0]) * y).astype(o_ref.dtype)


def mlstm_out(hf, hb, proj, norm_g, layer):
    bsz, n, w = hf.shape
    tm = ROW_TILE
    tile = pl.BlockSpec((1, tm, w), lambda b, t: (b, t, 0))
    return pl.pallas_call(
        _ya_kernel, grid=(bsz, n // tm),
        in_specs=[tile, tile, pl.BlockSpec((1, tm, w), lambda b, t: (b, t, COL_OA // w)),
                  pl.BlockSpec((None, 1, w), lambda b, t: (layer, 0, 0))],
        out_specs=tile, out_shape=jax.ShapeDtypeStruct((bsz, n, w), BF16),
        compiler_params=_params(("arbitrary", "arbitrary"), 40),
        name="mlstm_out",
    )(hf, hb, proj, norm_g.reshape(norm_g.shape[0], 1, w))


def _s5_kernel(uf_ref, ub_ref, bm_ref, cm_ref, a_ref, yf_ref, yb_ref, lhs_ref, re_ref, im_ref, y_ref, carry_ref):
    nb, t, _ = uf_ref.shape
    rows = t * nb
    sw = S5_BUNDLE * S5_STATE
    tiles_per_bundle = sw // LANES
    in_slabs = MIX_W // LANES

    @pl.when(pl.program_id(0) == 0)
    def _():
        carry_ref[...] = jnp.zeros_like(carry_ref)

    kb_per_half = S5_NB // S5_HALVES

    def interleave_inputs(d, u_ref):
        for b in range(nb):
            for s in range(in_slabs):
                lhs_ref[d, s, pl.ds(b, t, stride=nb), :] = u_ref[b, :, s * LANES:(s + 1) * LANES]

    def b_proj(d, kb):
        lhs = jnp.concatenate([lhs_ref[d, 2 * kb], lhs_ref[d, 2 * kb + 1]], axis=1).astype(BF16)
        bu = _dot(lhs, bm_ref[d, kb])
        half = kb // kb_per_half
        for lt in range(tiles_per_bundle):
            s = (kb % kb_per_half) * tiles_per_bundle + lt
            re_ref[d, s, pl.ds(half, rows, stride=S5_HALVES), :] = bu[:, lt * LANES:(lt + 1) * LANES]
            im_ref[d, s, pl.ds(half, rows, stride=S5_HALVES), :] = bu[:, sw + lt * LANES:sw + (lt + 1) * LANES]

    def c_proj(d, kb):
        half = kb // kb_per_half
        s0 = (kb % kb_per_half) * tiles_per_bundle
        sr = jnp.concatenate([re_ref[d, s0 + lt, pl.ds(half, rows, stride=S5_HALVES), :]
                              for lt in range(tiles_per_bundle)], axis=1).astype(BF16)
        si = jnp.concatenate([im_ref[d, s0 + lt, pl.ds(half, rows, stride=S5_HALVES), :]
                              for lt in range(tiles_per_bundle)], axis=1).astype(BF16)
        y = _dot(sr, cm_ref[kb, 0]) + _dot(si, cm_ref[kb, 1])
        y_ref[d, 2 * kb] = y[:, :LANES]
        y_ref[d, 2 * kb + 1] = y[:, LANES:]

    def scan_group(g, side_work):
        slabs = tuple(range(g * S5_GRP, (g + 1) * S5_GRP))
        xs = [carry_ref[d, c, s] for d in range(2) for s in slabs for c in range(2)]
        every = t // max(len(side_work), 1)
        for i in range(t):
            if side_work and i % every == 0 and i // every < len(side_work):
                side_work[i // every]()
            for d in range(2):
                r0 = (i if d == 0 else t - 1 - i) * SUBLANES
                for j, s in enumerate(slabs):
                    k = 2 * (d * S5_GRP + j)
                    xr, xi = xs[k], xs[k + 1]
                    ar, ai = a_ref[d, 0, s], a_ref[d, 1, s]
                    nr = ar * xr - ai * xi + re_ref[d, s, r0:r0 + SUBLANES, :]
                    ni = ar * xi + ai * xr + im_ref[d, s, r0:r0 + SUBLANES, :]
                    re_ref[d, s, r0:r0 + SUBLANES, :] = nr
                    im_ref[d, s, r0:r0 + SUBLANES, :] = ni
                    xs[k], xs[k + 1] = nr, ni
        for d in range(2):
            for j, s in enumerate(slabs):
                for c in range(2):
                    carry_ref[d, c, s] = xs[2 * (d * S5_GRP + j) + c]

    assert S5_SLABS // S5_GRP == kb_per_half == 2
    bundles = lambda g: [kb for kb in range(S5_NB) if kb % kb_per_half == g]
    for d, u_ref in enumerate((uf_ref, ub_ref)):
        interleave_inputs(d, u_ref)
        for kb in bundles(0):
            b_proj(d, kb)
    scan_group(0, [functools.partial(b_proj, d, kb) for d in range(2) for kb in bundles(1)])
    scan_group(1, [functools.partial(c_proj, d, kb) for d in range(2) for kb in bundles(0)])
    for d, out_ref in enumerate((yf_ref, yb_ref)):
        for kb in bundles(1):
            c_proj(d, kb)
        for b in range(nb):
            for s in range(in_slabs):
                out_ref[b, :, s * LANES:(s + 1) * LANES] = y_ref[d, s, pl.ds(b, t, stride=nb), :]


def s5_scan(proj, bmat, cmat, atab, n_ctx):
    bsz, n, _ = proj.shape
    assert bsz * S5_HALVES == SUBLANES, "the scan layout puts (sample, state half) on the 8 sublanes"
    t = S5_TC
    nc = n // t
    c0 = n_ctx // t
    w = MIX_W
    rows = t * bsz
    bwd = lambda j: _rev_chunk(j, c0, nc)
    out = jax.ShapeDtypeStruct((bsz, n, w), F32)
    const = lambda a: pl.BlockSpec(a.shape, lambda j: (0,) * a.ndim)
    return pl.pallas_call(
        _s5_kernel, grid=(nc,),
        in_specs=[pl.BlockSpec((bsz, t, w), lambda j: (0, j, COL_US // w)),
                  pl.BlockSpec((bsz, t, w), lambda j: (0, bwd(j), COL_US // w)),
                  const(bmat), const(cmat), const(atab)],
        out_specs=[pl.BlockSpec((bsz, t, w), lambda j: (0, j, 0)),
                   pl.BlockSpec((bsz, t, w), lambda j: (0, bwd(j), 0))],
        out_shape=[out, out],
        scratch_shapes=[pltpu.VMEM((2, w // LANES, rows, LANES), F32),
                        pltpu.VMEM((2, S5_SLABS, rows * S5_HALVES, LANES), F32),
                        pltpu.VMEM((2, S5_SLABS, rows * S5_HALVES, LANES), F32),
                        pltpu.VMEM((2, w // LANES, rows, LANES), F32),
                        pltpu.VMEM((2, 2, S5_SLABS, SUBLANES, LANES), F32)],
        compiler_params=_params(("arbitrary",), 56),
        name="s5_scan",
    )(proj, proj, bmat, cmat, atab)


def s5_operands(lam_re, lam_im, log_dt, b_re, b_im, c_re, c_im):
    lam_re, lam_im = lam_re.astype(F32), lam_im.astype(F32)
    dt = jnp.exp(log_dt.astype(F32))[..., None]
    mag = jnp.exp(lam_re * dt)
    a_re, a_im = mag * jnp.cos(lam_im * dt), mag * jnp.sin(lam_im * dt)
    nr, ni = a_re - 1.0, a_im
    den = lam_re * lam_re + lam_im * lam_im
    fr = (nr * lam_re + ni * lam_im) / den
    fi = (ni * lam_re - nr * lam_im) / den
    bb_re = fr[..., None] * b_re - fi[..., None] * b_im
    bb_im = fr[..., None] * b_im + fi[..., None] * b_re
    bw, sw = S5_BUNDLE * S5_GROUP, S5_BUNDLE * S5_STATE

    def blockdiag_b(bb):
        rows = bb.reshape(2, S5_NB, S5_BUNDLE, S5_STATE, S5_GROUP).transpose(0, 1, 2, 4, 3).reshape(
            2, S5_NB, bw, S5_STATE)
        on_diag = (jnp.arange(bw)[:, None] // S5_GROUP) == (jnp.arange(sw)[None, :] // S5_STATE)
        return jnp.where(on_diag, jnp.tile(rows, (1, 1, 1, S5_BUNDLE)), 0.0)

    bmat = jnp.concatenate([blockdiag_b(bb_re), blockdiag_b(bb_im)], -1).astype(BF16)

    def blockdiag_c(c):
        rows = c.astype(F32).reshape(S5_NB, S5_BUNDLE, S5_GROUP, S5_STATE).transpose(0, 1, 3, 2).reshape(
            S5_NB, sw, S5_GROUP)
        on_diag = (jnp.arange(sw)[:, None] // S5_STATE) == (jnp.arange(bw)[None, :] // S5_GROUP)
        return jnp.where(on_diag, jnp.tile(rows, (1, 1, S5_BUNDLE)), 0.0)

    cmat = jnp.stack([blockdiag_c(c_re), -blockdiag_c(c_im)], 1).astype(BF16)

    def slab_table(a):
        a4 = a.reshape(2, S5_HALVES, S5_SLABS, LANES)
        return jnp.take(a4, jnp.arange(SUBLANES) % S5_HALVES, axis=1).transpose(0, 2, 1, 3)

    atab = jnp.stack([slab_table(a_re), slab_table(a_im)], 1)
    return bmat, cmat, atab


def _glu_kernel(yf_ref, yb_ref, u_ref, d_ref, w_ref, b_ref, o_ref):
    y = d_ref[...] * u_ref[0] + yf_ref[0] + yb_ref[0]
    z = jax.nn.gelu(y)
    o_ref[0] = (z * jax.nn.sigmoid(_dot(z.astype(BF16), w_ref[...].astype(BF16)) + b_ref[...])).astype(o_ref.dtype)


def s5_glu(yf, yb, proj, d_skip, glu_w, glu_b, layer):
    bsz, n, w = yf.shape
    tm = ROW_TILE
    tile = pl.BlockSpec((1, tm, w), lambda b, t: (b, t, 0))
    vec = pl.BlockSpec((None, 1, w), lambda b, t: (layer, 0, 0))
    return pl.pallas_call(
        _glu_kernel, grid=(bsz, n // tm),
        in_specs=[tile, tile, pl.BlockSpec((1, tm, w), lambda b, t: (b, t, COL_US // w)),
                  vec, pl.BlockSpec((None, w, w), lambda b, t: (layer, 0, 0)), vec],
        out_specs=tile, out_shape=jax.ShapeDtypeStruct((bsz, n, w), BF16),
        compiler_params=_params(("arbitrary", "arbitrary"), 40),
        name="s5_glu",
    )(yf, yb, proj, d_skip.reshape(-1, 1, w), glu_w, glu_b.reshape(-1, 1, w))


def _da_prep_kernel(x_ref, g_ref, cos_ref, sin_ref, o_ref):
    cos = cos_ref[...]
    sin = sin_ref[...]
    lane = lax.broadcasted_iota(jnp.int32, cos.shape, 1)
    first = lane < DA_DQK
    quarter = DA_DQK // 4
    up = (lane & quarter) == 0
    scale = jnp.where(pl.program_id(2) == 0, DA_DQK ** -0.5 * math.log2(math.e), 1.0)
    slices = [slice(hd * LANES, (hd + 1) * LANES) for hd in range(DA_HEADS)]
    xs = [x_ref[0, :, sl] for sl in slices]
    sq = [x * x for x in xs]
    ms0 = [jnp.sum(jnp.where(first, s, 0.0), axis=-1, keepdims=True) for s in sq]
    ms1 = [jnp.sum(jnp.where(first, 0.0, s), axis=-1, keepdims=True) for s in sq]
    ys = [x * lax.rsqrt(jnp.where(first, a, b) * (1.0 / DA_DQK) + EPS) * g_ref[:, sl]
          for x, a, b, sl in zip(xs, ms0, ms1, slices)]
    partners = [jnp.where(up, pltpu.roll(y, LANES - quarter, 1), pltpu.roll(y, quarter, 1)) for y in ys]
    for y, p, sl in zip(ys, partners, slices):
        o_ref[0, :, sl] = ((y * cos + p * sin) * scale).astype(o_ref.dtype)


def da_prep(proj, qn_g, kn_g, cos_t, sin_t, layer):
    bsz, n, _ = proj.shape
    tm = ROW_TILE
    w = DA_HEADS * LANES
    reps = w // DA_DQK
    gains = jnp.concatenate([jnp.tile(qn_g, (1, reps)), jnp.tile(kn_g, (1, reps))], -1)
    gains = gains.reshape(gains.shape[0], 1, 2 * w)
    return pl.pallas_call(
        _da_prep_kernel, grid=(bsz, n // tm, 2),
        in_specs=[pl.BlockSpec((1, tm, w), lambda b, t, j: (b, t, COL_QD // w + j)),
                  pl.BlockSpec((None, 1, w), lambda b, t, j: (layer, 0, j)),
                  pl.BlockSpec((tm, LANES), lambda b, t, j: (t, 0)),
                  pl.BlockSpec((tm, LANES), lambda b, t, j: (t, 0))],
        out_specs=pl.BlockSpec((1, tm, w), lambda b, t, j: (b, t, j)),
        out_shape=jax.ShapeDtypeStruct((bsz, n, 2 * w), BF16),
        compiler_params=_params(("arbitrary", "arbitrary", "arbitrary"), 40),
        name="da_prep",
    )(proj, gains, cos_t, sin_t)


def rope_tables(n, n_ctx):
    lat = jnp.arange(n) - n_ctx
    rows = (lat // GRID_W).astype(F32)
    cols = (lat % GRID_W).astype(F32)
    lane = jnp.arange(LANES)
    in_comp = lane % DA_DQK
    quarter = DA_DQK // 4
    freqs = ROPE_BASE ** (-(in_comp % quarter).astype(F32) / quarter)
    pos = jnp.where((in_comp < DA_DQK // 2)[None, :], rows[:, None], cols[:, None])
    ang = pos * freqs[None, :]
    sign = jnp.where((in_comp % (2 * quarter)) < quarter, -1.0, 1.0)
    is_lat = (lat >= 0)[:, None]
    cos_t = jnp.where(is_lat, jnp.cos(ang), 1.0)
    sin_t = jnp.where(is_lat, jnp.sin(ang) * sign[None, :], 0.0)
    return cos_t.astype(F32), sin_t.astype(F32)


DA_HEADS_PER_STEP = 4


def _diff_attn_kernel(scal_ref, q_ref, k_ref, v_ref, g_ref, o_ref, *, n_ctx, ctx_tiles):
    lam = scal_ref[0]
    out_scale = scal_ref[1]
    lane = lax.broadcasted_iota(jnp.int32, (q_ref.shape[1], LANES), 1)
    first = lane < DA_DQK

    def exp_rows(s):
        e = jnp.exp2(s - jnp.max(s, axis=-1, keepdims=True))
        return e, jnp.sum(e, axis=-1, keepdims=True)

    def attend(nk):
        for hh in range(DA_HEADS_PER_STEP):
            sl = slice(hh * LANES, (hh + 1) * LANES)
            q = q_ref[0, :, sl]
            zero = jnp.zeros_like(q)
            k = k_ref[0, :nk, sl]
            e0, l0 = exp_rows(_dot_nt(jnp.where(first, q, zero), k))
            e1, l1 = exp_rows(_dot_nt(jnp.where(first, zero, q), k))
            a = e0 - (lam * l0 * (1.0 / l1)) * e1
            o = _dot(a.astype(BF16), v_ref[0, :nk, sl].astype(BF16)) * (1.0 / l0)
            o = o * lax.rsqrt(jnp.mean(o * o, axis=-1, keepdims=True) + EPS) * g_ref[...]
            o_ref[0, :, sl] = (o * out_scale).astype(o_ref.dtype)

    @pl.when(pl.program_id(2) < ctx_tiles)
    def _():
        attend(n_ctx)

    @pl.when(pl.program_id(2) >= ctx_tiles)
    def _():
        attend(k_ref.shape[1])


def diff_attn(qk_hat, proj, scal, sub_g, layer, n_ctx):
    bsz, n, _ = qk_hat.shape
    tq = SEQ_CHUNK
    w = DA_HEADS_PER_STEP * LANES
    return pl.pallas_call(
        functools.partial(_diff_attn_kernel, n_ctx=n_ctx, ctx_tiles=n_ctx // tq),
        grid=(bsz, DA_HEADS // DA_HEADS_PER_STEP, n // tq),
        in_specs=[pl.BlockSpec(memory_space=pltpu.SMEM),
                  pl.BlockSpec((1, tq, w), lambda b, h, t: (b, t, h)),
                  pl.BlockSpec((1, n, w), lambda b, h, t: (b, 0, DA_HEADS // DA_HEADS_PER_STEP + h)),
                  pl.BlockSpec((1, n, w), lambda b, h, t: (b, 0, COL_VD // w + h)),
                  pl.BlockSpec((None, 1, LANES), lambda b, h, t: (layer, 0, 0))],
        out_specs=pl.BlockSpec((1, tq, w), lambda b, h, t: (b, t, h)),
        out_shape=jax.ShapeDtypeStruct((bsz, n, DA_HEADS * DA_DV), BF16),
        compiler_params=_params(("arbitrary", "arbitrary", "arbitrary"), 48),
        name="diff_attn",
    )(scal, qk_hat, qk_hat, proj, sub_g.reshape(sub_g.shape[0], 1, DA_DV))


def _merge_kernel(ya_ref, yb_ref, yc_ref, p_ref, ga_ref, gb_ref, gc_ref, o_ref):
    acc = _dot(ya_ref[0], p_ref[0].astype(BF16)) * jax.nn.sigmoid(ga_ref[0])
    acc += _dot(yb_ref[0], p_ref[1].astype(BF16)) * jax.nn.sigmoid(gb_ref[0])
    acc += _dot(yc_ref[0], p_ref[2].astype(BF16)) * jax.nn.sigmoid(gc_ref[0])
    o_ref[0] = acc.astype(o_ref.dtype)


def branch_merge(ya, yb, yc, proj, branch_proj, layer):
    bsz, n, w = ya.shape
    d = branch_proj.shape[-1]
    tm, tn = 768, 512
    y_tile = pl.BlockSpec((1, tm, w), lambda j, b, t: (b, t, 0))
    gate = lambda r: pl.BlockSpec((1, tm, tn), lambda j, b, t: (b, t, (COL_GB + r * d) // tn + j))
    return pl.pallas_call(
        _merge_kernel, grid=(d // tn, bsz, n // tm),
        in_specs=[y_tile, y_tile, y_tile,
                  pl.BlockSpec((None, N_BRANCH, w, tn), lambda j, b, t: (layer, 0, 0, j)),
                  gate(0), gate(1), gate(2)],
        out_specs=pl.BlockSpec((1, tm, tn), lambda j, b, t: (b, t, j)),
        out_shape=jax.ShapeDtypeStruct((bsz, n, d), BF16),
        compiler_params=_params(("arbitrary", "arbitrary", "arbitrary"), 48),
        name="branch_merge",
    )(ya, yb, yc, branch_proj, proj, proj, proj)


INFO_E, INFO_POS, INFO_W = 0, 2, 4


def _pack_bf16_pair(x):
    w = x.shape[1] // 2
    lo = lax.bitcast_convert_type(x[:, :w].astype(BF16).astype(F32), jnp.uint32)
    hi = lax.bitcast_convert_type(x[:, w:].astype(BF16).astype(F32), jnp.uint32)
    return (lo >> 16) | (hi & jnp.uint32(0xFFFF0000))


def _unpack_bf16_pair(p):
    lo = lax.bitcast_convert_type(p << 16, F32)
    hi = lax.bitcast_convert_type(p & jnp.uint32(0xFFFF0000), F32)
    return jnp.concatenate([lo, hi], axis=1)


def _router_kernel(z_ref, g_ref, shift_ref, scale_ref, w_ref, b_ref, hp_ref, info_ref, cnt_ref):
    tm = z_ref.shape[1]

    @pl.when((pl.program_id(0) == 0) & (pl.program_id(1) == 0))
    def _():
        cnt_ref[...] = jnp.zeros_like(cnt_ref)

    z = z_ref[0]
    h = z * lax.rsqrt(jnp.mean(z * z, axis=-1, keepdims=True) + EPS) * g_ref[...]
    h = h * (1.0 + scale_ref[...]) + shift_ref[...]
    hp_ref[0] = _pack_bf16_pair(h)
    h_parts = _split3(h)
    w_parts = _split3(w_ref[...])
    logits = (_dot_nt(w_parts[0], h_parts[0]) + _dot_nt(w_parts[0], h_parts[1])
              + _dot_nt(w_parts[1], h_parts[0]))
    score = jax.nn.sigmoid(logits)
    sel = score + b_ref[...]
    rows = [sel[e:e + 1, :] for e in range(N_EXPERTS)]
    gscores = []
    for g in range(N_EXPERT_GROUPS):
        a, b, c, d = rows[g * EXPERTS_PER_GROUP:(g + 1) * EXPERTS_PER_GROUP]
        hi1, lo1 = jnp.maximum(a, b), jnp.minimum(a, b)
        hi2, lo2 = jnp.maximum(c, d), jnp.minimum(c, d)
        gscores.append(jnp.maximum(hi1, hi2) + jnp.maximum(jnp.minimum(hi1, hi2), jnp.maximum(lo1, lo2)))
    gmax = functools.reduce(jnp.maximum, gscores)
    taken = jnp.zeros_like(gmax, dtype=jnp.bool_)
    picked = []
    for g in range(N_EXPERT_GROUPS):
        is_g = (gscores[g] == gmax) & jnp.logical_not(taken)
        picked.append(is_g)
        taken = taken | is_g
    chosen = []
    for e in range(N_EXPERTS):
        g = e // EXPERTS_PER_GROUP
        rank = jnp.zeros_like(gmax)
        for o in range(g * EXPERTS_PER_GROUP, (g + 1) * EXPERTS_PER_GROUP):
            if o == e:
                continue
            ahead = (rows[o] > rows[e]) | ((rows[o] == rows[e]) & (o < e))
            rank = rank + ahead.astype(F32)
        chosen.append(picked[g] & (rank < float(TOP_K)))
    onehot = jnp.concatenate([c.astype(F32) for c in chosen], axis=0)
    total = jnp.sum(onehot * score, axis=0, keepdims=True)
    gates = onehot * score * (1.0 / total)

    r_i = lax.broadcasted_iota(jnp.int32, (tm, tm), 0)
    c_i = lax.broadcasted_iota(jnp.int32, (tm, tm), 1)
    incl = _dot(onehot.astype(BF16), (r_i <= c_i).astype(BF16))
    pos = incl - onehot + cnt_ref[:, :1]
    cnt_ref[...] = cnt_ref[...] + incl[:, tm - 1:tm]

    seen = jnp.zeros_like(gmax)
    rec = [jnp.zeros_like(gmax) for _ in range(3 * TOP_K)]
    for e in range(N_EXPERTS):
        oh = onehot[e:e + 1, :]
        for k, is_k in enumerate((oh * (1.0 - seen), oh * seen)):
            rec[INFO_E + k] += is_k * float(e)
            rec[INFO_POS + k] += is_k * pos[e:e + 1, :]
            rec[INFO_W + k] += is_k * gates[e:e + 1, :]
        seen = seen + oh
    pad = jnp.zeros((LANES - len(rec), tm), F32)
    info_ref[0] = jnp.concatenate(rec + [pad], axis=0).T


def moe_router(z, norm_g, mods, layer, n_ctx, router_w, router_b):
    bsz, n, d = z.shape
    tm = SEQ_CHUNK
    ctx_tiles = n_ctx // tm
    return pl.pallas_call(
        _router_kernel, grid=(bsz, n // tm),
        in_specs=[pl.BlockSpec((1, tm, d), lambda b, t: (b, t, 0)),
                  pl.BlockSpec((None, 1, d), lambda b, t: (layer, 0, 0)),
                  _mod_spec(layer, 3, bsz, ctx_tiles), _mod_spec(layer, 4, bsz, ctx_tiles),
                  pl.BlockSpec((N_EXPERTS, d), lambda b, t: (0, 0)),
                  pl.BlockSpec((N_EXPERTS, 1), lambda b, t: (0, 0))],
        out_specs=[pl.BlockSpec((1, tm, d // 2), lambda b, t: (b, t, 0)),
                   pl.BlockSpec((1, tm, LANES), lambda b, t: (b, t, 0)),
                   pl.BlockSpec((N_EXPERTS, LANES), lambda b, t: (0, 0))],
        out_shape=[jax.ShapeDtypeStruct((bsz, n, d // 2), jnp.uint32),
                   jax.ShapeDtypeStruct((bsz, n, LANES), F32),
                   jax.ShapeDtypeStruct((N_EXPERTS, LANES), F32)],
        compiler_params=_params(("arbitrary", "arbitrary"), 40),
        name="moe_router",
    )(z, norm_g.reshape(norm_g.shape[0], 1, d), mods, mods, router_w.T, router_b.reshape(N_EXPERTS, 1))


def moe_plan(info, counts, n_tiles):
    m = info.shape[0] * info.shape[1]
    rec = info.reshape(m, LANES)
    cnt = counts[:, 0].astype(jnp.int32)
    padded = (cnt + MOE_TILE - 1) // MOE_TILE * MOE_TILE
    ends = jnp.cumsum(padded)
    starts = ends - padded
    experts = rec[:, INFO_E:INFO_E + TOP_K].astype(jnp.int32)
    dest = starts[experts] + rec[:, INFO_POS:INFO_POS + TOP_K].astype(jnp.int32)
    tile_start = jnp.arange(n_tiles, dtype=jnp.int32) * MOE_TILE
    tile_expert = jnp.minimum(jnp.sum(ends[None, :] <= tile_start[:, None], axis=1), N_EXPERTS - 1).astype(jnp.int32)
    n_used = (ends[-1] // MOE_TILE).astype(jnp.int32).reshape(1)
    return dest.reshape(-1), tile_expert, n_used


def _row_copy(src_ref, src_row, dst_ref, dst_row, sem):
    return pltpu.make_async_copy(src_ref.at[pl.ds(src_row, 1), :], dst_ref.at[pl.ds(dst_row, 1), :], sem)


def _dispatch_kernel(dest_ref, h_ref, xs_in_ref, xs_ref, sem):
    del xs_in_ref
    tm = h_ref.shape[0]
    base = pl.program_id(0) * tm

    def start(r, carry):
        for k in range(TOP_K):
            _row_copy(h_ref, r, xs_ref, dest_ref[(base + r) * TOP_K + k], sem).start(priority=k)
        return carry

    def wait(r, carry):
        for k in range(TOP_K):
            _row_copy(h_ref, 0, xs_ref, 0, sem).wait()
        return carry

    lax.fori_loop(0, tm, start, 0, unroll=DMA_UNROLL)
    lax.fori_loop(0, tm, wait, 0, unroll=DMA_UNROLL)


def moe_dispatch(h2, dest, xs_init):
    m, d = h2.shape
    n_rows = xs_init.shape[0]
    tm = SEQ_CHUNK
    grid_spec = pltpu.PrefetchScalarGridSpec(
        num_scalar_prefetch=1, grid=(m // tm,),
        in_specs=[pl.BlockSpec((tm, d), lambda i, dest: (i, 0)),
                  pl.BlockSpec(memory_space=pl.ANY)],
        out_specs=pl.BlockSpec(memory_space=pl.ANY),
        scratch_shapes=[pltpu.SemaphoreType.DMA])
    return pl.pallas_call(
        _dispatch_kernel, grid_spec=grid_spec,
        out_shape=jax.ShapeDtypeStruct((n_rows, d), h2.dtype),
        input_output_aliases={2: 0},
        compiler_params=_params(("arbitrary",), 40),
        name="moe_dispatch",
    )(dest, h2, xs_init)


def _expert_kernel(te_ref, nu_ref, x_ref, wg_ref, wu_ref, wd_ref, o_ref, wg_scr, wu_scr, wd_scr):
    i = pl.program_id(0)
    used = i < nu_ref[0]

    @pl.when(used & ((i == 0) | (te_ref[i] != te_ref[jnp.maximum(i - 1, 0)])))
    def _():
        wg_scr[...] = wg_ref[...].astype(BF16)
        wu_scr[...] = wu_ref[...].astype(BF16)
        wd_scr[...] = wd_ref[...].astype(BF16)

    @pl.when(used)
    def _():
        x = _unpack_bf16_pair(x_ref[...]).astype(BF16)
        a = _dot(x, wg_scr[...])
        act = (a * jax.nn.sigmoid(a)) * _dot(x, wu_scr[...])
        o_ref[...] = _pack_bf16_pair(_dot(act.astype(BF16), wd_scr[...]))

    @pl.when(jnp.logical_not(used))
    def _():
        o_ref[...] = jnp.zeros_like(o_ref)


def moe_experts(xs, tile_expert, n_used, w_gate, w_up, w_down, layer):
    n_rows, dp = xs.shape
    d, f = w_gate.shape[-2:]
    w_spec = lambda r, c: pl.BlockSpec((None, None, r, c), lambda i, te, nu: (layer, te[i], 0, 0))
    grid_spec = pltpu.PrefetchScalarGridSpec(
        num_scalar_prefetch=2, grid=(n_rows // MOE_TILE,),
        in_specs=[pl.BlockSpec((MOE_TILE, dp), lambda i, te, nu: (i, 0)),
                  w_spec(d, f), w_spec(d, f), w_spec(f, d)],
        out_specs=pl.BlockSpec((MOE_TILE, dp), lambda i, te, nu: (i, 0)),
        scratch_shapes=[pltpu.VMEM((d, f), BF16), pltpu.VMEM((d, f), BF16), pltpu.VMEM((f, d), BF16)])
    return pl.pallas_call(
        _expert_kernel, grid_spec=grid_spec,
        out_shape=jax.ShapeDtypeStruct((n_rows, dp), jnp.uint32),
        compiler_params=_params(("arbitrary",), 56),
        name="moe_experts",
    )(tile_expert, n_used, xs, w_gate, w_up, w_down)


def _combine_kernel(dest_ref, ys_ref, info_ref, z_ref, gate_ref, *rest, with_norm):
    if with_norm:
        g_ref, shift_ref, scale_ref, o_ref, h_ref, buf_ref, sem = rest
    else:
        o_ref, buf_ref, sem = rest
    tm = z_ref.shape[1]
    base = (pl.program_id(0) * pl.num_programs(1) + pl.program_id(1)) * tm

    def start(r, carry):
        for k in range(TOP_K):
            _row_copy(ys_ref, dest_ref[(base + r) * TOP_K + k], buf_ref.at[k], r, sem).start(priority=k)
        return carry

    def wait(r, carry):
        for k in range(TOP_K):
            _row_copy(ys_ref, 0, buf_ref.at[k], 0, sem).wait()
        return carry

    lax.fori_loop(0, tm, start, 0, unroll=DMA_UNROLL)
    lax.fori_loop(0, tm, wait, 0, unroll=DMA_UNROLL)
    info = info_ref[0]
    mix = (info[:, INFO_W:INFO_W + 1] * _unpack_bf16_pair(buf_ref[0])
           + info[:, INFO_W + 1:INFO_W + 2] * _unpack_bf16_pair(buf_ref[1]))
    z = z_ref[0] + gate_ref[...] * mix
    o_ref[0] = z
    if with_norm:
        y = z * lax.rsqrt(jnp.mean(z * z, axis=-1, keepdims=True) + EPS) * g_ref[...]
        h_ref[0] = (y * (1.0 + scale_ref[...]) + shift_ref[...]).astype(h_ref.dtype)


def moe_combine(ys, dest, info, z, mods, layer, gate_chunk, n_ctx, next_norm_g=None):
    bsz, n, d = z.shape
    tm = SEQ_CHUNK
    ctx_tiles = n_ctx // tm
    with_norm = next_norm_g is not None
    tile = pl.BlockSpec((1, tm, d), lambda b, t, dest: (b, t, 0))
    mod_row = lambda lyr, chunk: pl.BlockSpec(
        (None, None, 1, d), lambda b, t, dest: (lyr, jnp.where(t < ctx_tiles, bsz, b), 0, chunk))
    in_specs = [pl.BlockSpec(memory_space=pl.ANY),
                pl.BlockSpec((1, tm, LANES), lambda b, t, dest: (b, t, 0)),
                tile, mod_row(layer, gate_chunk)]
    args = [dest, ys, info, z, mods]
    out_specs, out_shape = tile, jax.ShapeDtypeStruct(z.shape, F32)
    if with_norm:
        in_specs += [pl.BlockSpec((None, 1, d), lambda b, t, dest: (layer + 1, 0, 0)),
                     mod_row(layer + 1, 0), mod_row(layer + 1, 1)]
        args += [next_norm_g.reshape(next_norm_g.shape[0], 1, d), mods, mods]
        out_specs, out_shape = [tile, tile], [out_shape, jax.ShapeDtypeStruct(z.shape, BF16)]
    grid_spec = pltpu.PrefetchScalarGridSpec(
        num_scalar_prefetch=1, grid=(bsz, n // tm), in_specs=in_specs, out_specs=out_specs,
        scratch_shapes=[pltpu.VMEM((TOP_K, tm, ys.shape[1]), ys.dtype), pltpu.SemaphoreType.DMA])
    return pl.pallas_call(
        functools.partial(_combine_kernel, with_norm=with_norm), grid_spec=grid_spec, out_shape=out_shape,
        compiler_params=_params(("arbitrary", "arbitrary"), 40),
        name="moe_combine",
    )(*args)


def kernel(x, c, ctx, c_ctx, norm1_g, norm2_g, ada_w, ada_b, w_in, b_in, ml_conv_w, ml_conv_b, ml_norm_g, s5_lam_re, s5_lam_im, s5_log_dt, s5_b_re, s5_b_im, s5_c_re, s5_c_im, s5_d, s5_glu_w, s5_glu_b, da_q_norm_g, da_k_norm_g, da_lam_q1, da_lam_k1, da_lam_q2, da_lam_k2, da_sub_norm_g, branch_proj, w_out, router_w, router_b, exp_w_gate, exp_w_up, exp_w_down):
    bsz, n_lat, d = x.shape
    n_ctx = ctx.shape[1]
    n = n_ctx + n_lat
    depth = w_in.shape[0]
    assert n_ctx % SEQ_CHUNK == 0 and n_lat % SEQ_CHUNK == 0 and d == D_MODEL

    rows = -(-(bsz + 1) // SUBLANES) * SUBLANES
    cvec = jnp.zeros((rows, d), F32).at[:bsz].set(c).at[bsz].set(c_ctx)
    mods = ada_mod(cvec, ada_w, ada_b).reshape(depth, rows, 1, 6 * d)

    w_in_t = jnp.swapaxes(w_in, 1, 2)
    b_in3 = b_in[:, None, :]
    b_ml = b_in3[..., :GATE_OFF]
    b_rest = b_in3[..., REST_OFF:]
    m_tok = bsz * n
    moe_tiles = m_tok * TOP_K // MOE_TILE + N_EXPERTS
    xs = jnp.zeros((moe_tiles * MOE_TILE, d // 2), jnp.uint32)
    cos_t, sin_t = rope_tables(n, n_ctx)
    z = jnp.concatenate([ctx, x], 1)
    (h,) = norm_mod(z, norm1_g, mods, 0, 0, 1, n_ctx, (BF16,))
    for i in range(depth):
        lam_init = 0.8 - 0.6 * math.exp(-0.3 * i)
        proj_ml = in_proj(h, w_in_t, b_ml, i, 0)
        proj = in_proj(h, w_in_t, b_rest, i, REST_OFF)
        gates = gate_proj(h, w_in_t, b_in3, i, GATE_OFF)
        gates_t = jnp.swapaxes(gates[..., :N_GATES], 1, 2)

        qk, qkt = conv_silu(proj_ml, ml_conv_w, ml_conv_b, i, n_ctx)
        hf, hb = mlstm(qk, qkt, proj_ml, gates, gates_t, n_ctx)
        ya = mlstm_out(hf, hb, proj_ml, ml_norm_g, i)

        bmat, cmat, atab = s5_operands(s5_lam_re[i], s5_lam_im[i], s5_log_dt[i], s5_b_re[i], s5_b_im[i],
                                       s5_c_re[i], s5_c_im[i])
        yf, ybk = s5_scan(proj, bmat, cmat, atab, n_ctx)
        yb = s5_glu(yf, ybk, proj, s5_d, s5_glu_w, s5_glu_b, i)

        qk_hat = da_prep(proj, da_q_norm_g, da_k_norm_g, cos_t, sin_t, i)
        lam = (jnp.exp(jnp.sum(da_lam_q1[i].astype(F32) * da_lam_k1[i].astype(F32)))
               - jnp.exp(jnp.sum(da_lam_q2[i].astype(F32) * da_lam_k2[i].astype(F32))) + lam_init)
        scal = jnp.stack([lam, jnp.asarray(1.0 - lam_init, F32)]).astype(F32)
        yc = diff_attn(qk_hat, proj, scal, da_sub_norm_g, i, n_ctx)

        merged = branch_merge(ya, yb, yc, proj, branch_proj, i)
        z = matmul_resid(merged, w_out, z, mods, i, 2, n_ctx)

        h2p, info, counts = moe_router(z, norm2_g, mods, i, n_ctx, router_w, router_b)
        dest, tile_expert, n_used = moe_plan(info, counts, moe_tiles)
        xs = moe_dispatch(h2p.reshape(m_tok, d // 2), dest, xs)
        ys = moe_experts(xs, tile_expert, n_used, exp_w_gate, exp_w_up, exp_w_down, i)
        if i + 1 < depth:
            z, h = moe_combine(ys, dest, info, z, mods, i, 5, n_ctx, next_norm_g=norm1_g)
        else:
            z = moe_combine(ys, dest, info, z, mods, i, 5, n_ctx)
    return z[:, n_ctx:]
```

```python
import functools
import math

import jax
import jax.numpy as jnp
from jax import lax
from jax.experimental import pallas as pl
from jax.experimental.pallas import tpu as pltpu

F32 = jnp.float32
BF16 = jnp.bfloat16

D_MODEL = 2048
GRID_W = 64
MIX_W = 1024
N_BRANCH = 3
ML_HEADS = 4
ML_DH = MIX_W // ML_HEADS
S5_GROUP = 16
S5_GROUPS = MIX_W // S5_GROUP
S5_STATE = 64
S5_BUNDLE = 16
S5_NB = S5_GROUPS // S5_BUNDLE
S5_NS = S5_GROUPS * S5_STATE
DA_HEADS = 8
DA_DQK = 64
DA_DV = 2 * DA_DQK
ROPE_BASE = 10000.0
N_EXPERTS = 16
N_EXPERT_GROUPS = 4
EXPERTS_PER_GROUP = N_EXPERTS // N_EXPERT_GROUPS
TOP_K = 2
D_FF_EXPERT = 512
EPS = 1e-6

COL_QA, COL_KA, COL_VA, COL_OA = 0, 1024, 2048, 3072
COL_US, COL_QD, COL_KD, COL_VD, COL_GB = 0, 1024, 2048, 3072, 4096
GATE_OFF = 4 * MIX_W
N_GATES = 4 * ML_HEADS
REST_OFF = GATE_OFF + N_GATES
REST_COLS = 4 * MIX_W + N_BRANCH * D_MODEL

SEQ_CHUNK = 256
ROW_TILE = 768
LANES = 128
SUBLANES = 8
MIB = 1024 * 1024

S5_TC = 64
S5_HALVES = 2
S5_SLABS = S5_NS // S5_HALVES // LANES
S5_GRP = 8
MOE_TILE = 512
DMA_UNROLL = 16


def _params(sem, vmem_mib):
    return pltpu.CompilerParams(dimension_semantics=sem, vmem_limit_bytes=vmem_mib * MIB)


def _dot(a, b):
    return jnp.dot(a, b, preferred_element_type=F32)


def _dot_nt(a, b):
    return lax.dot_general(a, b, (((1,), (1,)), ((), ())), preferred_element_type=F32)


def _split3(x):
    hi = x.astype(BF16)
    r = x - hi.astype(F32)
    mid = r.astype(BF16)
    lo = (r - mid.astype(F32)).astype(BF16)
    return hi, mid, lo


def _rev_chunk(j, n_ctx_chunks, n_chunks):
    return jnp.where(j < n_ctx_chunks, n_ctx_chunks - 1 - j, n_chunks - 1 - (j - n_ctx_chunks))


def _ada_kernel(c_ref, w_ref, b_ref, o_ref):
    c = c_ref[...]
    s = (c * jax.nn.sigmoid(c)).astype(BF16)
    o_ref[0] = _dot(s, w_ref[0].astype(BF16)) + b_ref[0]


def ada_mod(cvec, ada_w, ada_b):
    depth, d, n6 = ada_w.shape
    rows = cvec.shape[0]
    tn = 1024
    return pl.pallas_call(
        _ada_kernel,
        grid=(depth, n6 // tn),
        in_specs=[pl.BlockSpec((rows, d), lambda l, j: (0, 0)),
                  pl.BlockSpec((1, d, tn), lambda l, j: (l, 0, j)),
                  pl.BlockSpec((1, 1, tn), lambda l, j: (l, 0, j))],
        out_specs=pl.BlockSpec((1, rows, tn), lambda l, j: (l, 0, j)),
        out_shape=jax.ShapeDtypeStruct((depth, rows, n6), F32),
        compiler_params=_params(("arbitrary", "arbitrary"), 40),
        name="ada_mod",
    )(cvec, ada_w, ada_b.reshape(depth, 1, n6))


def _mod_spec(layer, chunk, batch, ctx_tiles):
    return pl.BlockSpec((None, None, 1, D_MODEL),
                        lambda b, t: (layer, jnp.where(t < ctx_tiles, batch, b), 0, chunk))


def _norm_mod_kernel(z_ref, g_ref, shift_ref, scale_ref, *outs):
    z = z_ref[0]
    y = z * lax.rsqrt(jnp.mean(z * z, axis=-1, keepdims=True) + EPS) * g_ref[...]
    h = y * (1.0 + scale_ref[...]) + shift_ref[...]
    for o in outs:
        o[0] = h.astype(o.dtype)


def norm_mod(z, norm_g, mods, layer, shift_chunk, scale_chunk, n_ctx, out_dtypes):
    bsz, n, d = z.shape
    tm = SEQ_CHUNK
    ctx_tiles = n_ctx // tm
    tile = pl.BlockSpec((1, tm, d), lambda b, t: (b, t, 0))
    return pl.pallas_call(
        _norm_mod_kernel, grid=(bsz, n // tm),
        in_specs=[tile, pl.BlockSpec((None, 1, d), lambda b, t: (layer, 0, 0)),
                  _mod_spec(layer, shift_chunk, bsz, ctx_tiles), _mod_spec(layer, scale_chunk, bsz, ctx_tiles)],
        out_specs=[tile] * len(out_dtypes),
        out_shape=[jax.ShapeDtypeStruct(z.shape, dt) for dt in out_dtypes],
        compiler_params=_params(("arbitrary", "arbitrary"), 40),
        name="norm_mod",
    )(z, norm_g.reshape(norm_g.shape[0], 1, d), mods, mods)


def _in_proj_kernel(x_ref, wt_ref, b_ref, o_ref, w_scr):
    @pl.when((pl.program_id(1) == 0) & (pl.program_id(2) == 0))
    def _():
        rows = 256
        for r in range(0, wt_ref.shape[0], rows):
            w_scr[:, r:r + rows] = wt_ref[r:r + rows, :].T.astype(BF16)

    o_ref[0] = (_dot(x_ref[0], w_scr[...]) + b_ref[...]).astype(o_ref.dtype)


def in_proj(x, w_t, bias, layer, row0, out_dtype=F32):
    bsz, n, k = x.shape
    n_cols = bias.shape[-1]
    tm, tn = n // 2, 1024
    return pl.pallas_call(
        _in_proj_kernel, grid=(n_cols // tn, bsz, n // tm),
        in_specs=[pl.BlockSpec((1, tm, k), lambda j, b, t: (b, t, 0)),
                  pl.BlockSpec((None, pl.Element(tn), pl.Element(k)), lambda j, b, t: (layer, pl.multiple_of(row0 + j * tn, SUBLANES), 0)),
                  pl.BlockSpec((None, 1, tn), lambda j, b, t: (layer, 0, j))],
        out_specs=pl.BlockSpec((1, tm, tn), lambda j, b, t: (b, t, j)),
        out_shape=jax.ShapeDtypeStruct((bsz, n, n_cols), out_dtype),
        scratch_shapes=[pltpu.VMEM((k, tn), BF16)],
        compiler_params=_params(("arbitrary", "arbitrary", "arbitrary"), 56),
        name="in_proj",
    )(x, w_t, bias)


def _gates_kernel(x_ref, wt_ref, b_ref, o_ref):
    o_ref[0] = _dot_nt(x_ref[0], wt_ref[...].astype(BF16)) + b_ref[...]


def gate_proj(x, w_t, bias, layer, row0):
    bsz, n, k = x.shape
    tm = n // 2
    rb = row0 // LANES
    return pl.pallas_call(
        _gates_kernel, grid=(bsz, n // tm),
        in_specs=[pl.BlockSpec((1, tm, k), lambda b, t: (b, t, 0)),
                  pl.BlockSpec((None, LANES, k), lambda b, t: (layer, rb, 0)),
                  pl.BlockSpec((None, 1, LANES), lambda b, t: (layer, 0, rb))],
        out_specs=pl.BlockSpec((1, tm, LANES), lambda b, t: (b, t, 0)),
        out_shape=jax.ShapeDtypeStruct((bsz, n, LANES), F32),
        compiler_params=_params(("arbitrary", "arbitrary"), 40),
        name="gate_proj",
    )(x, w_t, bias)


def _mm_resid_kernel(x_ref, w_ref, z_ref, gctx_ref, glat_ref, o_ref, *, n_ctx):
    acc = _dot(x_ref[0].astype(BF16), w_ref[...].astype(BF16))
    tm = acc.shape[0]
    row = pl.program_id(2) * tm + lax.broadcasted_iota(jnp.int32, (tm, 1), 0)
    gate = jnp.where(row < n_ctx, gctx_ref[...], glat_ref[...])
    o_ref[0] = z_ref[0] + gate * acc


def matmul_resid(x, w, z, mods, layer, gate_chunk, n_ctx):
    bsz, n, k = x.shape
    nc = w.shape[-1]
    tm, tn = 768, 1024
    gate_spec = lambda ctx: pl.BlockSpec(
        (None, None, 1, tn),
        lambda j, b, t: (layer, bsz if ctx else b, 0, gate_chunk * (nc // tn) + j))
    return pl.pallas_call(
        functools.partial(_mm_resid_kernel, n_ctx=n_ctx), grid=(nc // tn, bsz, n // tm),
        in_specs=[pl.BlockSpec((1, tm, k), lambda j, b, t: (b, t, 0)),
                  pl.BlockSpec((None, k, tn), lambda j, b, t: (layer, 0, j)),
                  pl.BlockSpec((1, tm, tn), lambda j, b, t: (b, t, j)),
                  gate_spec(True), gate_spec(False)],
        out_specs=pl.BlockSpec((1, tm, tn), lambda j, b, t: (b, t, j)),
        out_shape=jax.ShapeDtypeStruct((bsz, n, nc), F32),
        compiler_params=_params(("arbitrary", "arbitrary", "arbitrary"), 48),
        name="matmul_resid",
    )(x, w, z, mods, mods)


def _conv_kernel(x_ref, w_ref, b_ref, o_ref, ot_ref, *, n_ctx):
    x = x_ref[0]
    n = x.shape[0]
    row = lax.broadcasted_iota(jnp.int32, x.shape, 0)
    prev = jnp.where((row == 0) | (row == n_ctx), 0.0, pltpu.roll(x, 1, 0))
    nxt = jnp.where((row == n_ctx - 1) | (row == n - 1), 0.0, pltpu.roll(x, n - 1, 0))
    w = w_ref[...]
    y = b_ref[...] + prev * w[0:1] + x * w[1:2] + nxt * w[2:3]
    y = y * jax.nn.sigmoid(y)
    k_tiles = pl.num_programs(1) // 2
    y = y * jnp.where(pl.program_id(1) >= k_tiles, ML_DH ** -0.5, 1.0)
    o_ref[0] = y.astype(o_ref.dtype)
    ot_ref[0] = y.T.astype(ot_ref.dtype)


def conv_silu(proj, conv_w, conv_b, layer, n_ctx):
    bsz, n, _ = proj.shape
    tc = 256
    c2 = 2 * MIX_W
    return pl.pallas_call(
        functools.partial(_conv_kernel, n_ctx=n_ctx),
        grid=(bsz, c2 // tc),
        in_specs=[pl.BlockSpec((1, n, tc), lambda b, c: (b, 0, c)),
                  pl.BlockSpec((None, conv_w.shape[1], tc), lambda b, c: (layer, 0, c)),
                  pl.BlockSpec((None, 1, tc), lambda b, c: (layer, 0, c))],
        out_specs=[pl.BlockSpec((1, n, tc), lambda b, c: (b, 0, c)),
                   pl.BlockSpec((1, tc, n), lambda b, c: (b, c, 0))],
        out_shape=[jax.ShapeDtypeStruct((bsz, n, c2), BF16),
                   jax.ShapeDtypeStruct((bsz, c2, n), BF16)],
        compiler_params=_params(("arbitrary", "arbitrary"), 48),
        name="conv_silu",
    )(proj, conv_w, conv_b.reshape(conv_b.shape[0], 1, c2))


def _log_sigmoid(x):
    return jnp.minimum(x, 0.0) - jnp.log(1.0 + jnp.exp(-jnp.abs(x)))


def _mlstm_kernel(qf_ref, kf_ref, ktf_ref, vf_ref, gf_ref, gtf_ref,
                  qb_ref, kb_ref, ktb_ref, vb_ref, gb_ref, gtb_ref,
                  hf_ref, hb_ref, c_ref, m_ref):
    t = qf_ref.shape[1]
    dh = ML_DH

    @pl.when(pl.program_id(1) == 0)
    def _():
        c_ref[...] = jnp.zeros_like(c_ref)
        m_ref[...] = jnp.zeros_like(m_ref)

    r_i = lax.broadcasted_iota(jnp.int32, (t, t), 0)
    c_i = lax.broadcasted_iota(jnp.int32, (t, t), 1)
    low = c_i <= r_i
    upp = c_i >= r_i
    low_b = low.astype(BF16)
    upp_b = upp.astype(BF16)
    ones_col = (lax.broadcasted_iota(jnp.int32, (t, LANES), 1) == 0).astype(F32)

    dirs = ((qf_ref, kf_ref, ktf_ref, vf_ref, gf_ref, gtf_ref, hf_ref, low, low_b, upp_b),
            (qb_ref, kb_ref, ktb_ref, vb_ref, gb_ref, gtb_ref, hb_ref, upp, upp_b, low_b))
    for d, (q_ref, k_ref, kt_ref, v_ref, g_ref, gt_ref, h_ref, mask, col_tri, row_tri) in enumerate(dirs):
        g = g_ref[0]
        gt = gt_ref[0]
        lf = _log_sigmoid(g)
        lft = _log_sigmoid(gt)
        b_cols = sum(_dot(col_tri, p) for p in _split3(lf))
        b_rows = sum(_dot(p, row_tri) for p in _split3(lft))
        last = t - 1 if d == 0 else 0
        for hd in range(ML_HEADS):
            ci = 2 * d * ML_HEADS + hd
            cf = ci + ML_HEADS
            idx = d * ML_HEADS + hd
            sl = slice(hd * dh, (hd + 1) * dh)
            q = q_ref[0, :, sl]
            k = k_ref[0, :, sl]
            kt = kt_ref[0, sl, :]
            v = v_ref[0, :, sl]
            i_col = g[:, ci:ci + 1]
            i_row = gt[ci:ci + 1, :]
            b_col = b_cols[:, cf:cf + 1]
            b_row = b_rows[cf:cf + 1, :]
            b_tot = b_col[last:last + 1, :]
            m_prev = m_ref[idx][:, :1]
            c_prev = c_ref[idx]

            log_inter = b_col + m_prev
            log_intra = jnp.where(mask, b_col - b_row + i_row, -jnp.inf)
            m_t = jnp.maximum(log_inter, jnp.max(log_intra, axis=-1, keepdims=True))
            w_inter = jnp.exp(log_inter - m_t)
            s = _dot_nt(q, k) * jnp.exp(log_intra - m_t)
            v_ext = jnp.concatenate([v, ones_col], axis=1)
            nd = w_inter * _dot(q, c_prev.astype(BF16)) + _dot(s.astype(BF16), v_ext.astype(BF16))
            den = nd[:, dh:dh + 1]
            h_ref[0, :, sl] = nd[:, :dh] * (1.0 / jnp.maximum(jnp.abs(den), jnp.exp(-m_t)))

            log_w = b_tot - b_col + i_col
            m_new = jnp.maximum(b_tot + m_prev, jnp.max(log_w, axis=0, keepdims=True))
            decay = jnp.exp(b_tot + m_prev - m_new)
            w_s = jnp.exp(log_w - m_new)
            c_ref[idx] = decay * c_prev + _dot(kt, (v_ext * w_s).astype(BF16))
            m_ref[idx] = jnp.broadcast_to(m_new, (1, LANES))


def mlstm(qk, qkt, proj, gates, gates_t, n_ctx):
    bsz, n, _ = qk.shape
    t = SEQ_CHUNK
    nc = n // t
    c0 = n_ctx // t
    w = MIX_W
    fwd = lambda j: j
    bwd = lambda j: _rev_chunk(j, c0, nc)

    def specs(ch):
        return [pl.BlockSpec((1, t, w), lambda b, j: (b, ch(j), 0)),
                pl.BlockSpec((1, t, w), lambda b, j: (b, ch(j), 1)),
                pl.BlockSpec((1, w, t), lambda b, j: (b, 1, ch(j))),
                pl.BlockSpec((1, t, w), lambda b, j: (b, ch(j), COL_VA // w)),
                pl.BlockSpec((1, t, LANES), lambda b, j: (b, ch(j), 0)),
                pl.BlockSpec((1, N_GATES, t), lambda b, j: (b, 0, ch(j)))]

    args = [qk, qk, qkt, proj, gates, gates_t]
    out = jax.ShapeDtypeStruct((bsz, n, w), F32)
    return pl.pallas_call(
        _mlstm_kernel, grid=(bsz, nc),
        in_specs=specs(fwd) + specs(bwd),
        out_specs=[pl.BlockSpec((1, t, w), lambda b, j: (b, fwd(j), 0)),
                   pl.BlockSpec((1, t, w), lambda b, j: (b, bwd(j), 0))],
        out_shape=[out, out],
        scratch_shapes=[pltpu.VMEM((2 * ML_HEADS, ML_DH, ML_DH + LANES), F32),
                        pltpu.VMEM((2 * ML_HEADS, 1, LANES), F32)],
        compiler_params=_params(("arbitrary", "arbitrary"), 48),
        name="mlstm",
    )(*(args + args))


def _ya_kernel(hf_ref, hb_ref, oa_ref, g_ref, o_ref):
    h = hf_ref[0] + hb_ref[0]
    parts = []
    for hd in range(ML_HEADS):
        x = h[:, hd * ML_DH:(hd + 1) * ML_DH]
        parts.append(x * lax.rsqrt(jnp.mean(x * x, axis=-1, keepdims=True) + EPS))
    y = jnp.concatenate(parts, axis=1) * g_ref[...]
    o_ref[0] = (jax.nn.sigmoid(oa_ref[0]) * y).astype(o_ref.dtype)


def mlstm_out(hf, hb, proj, norm_g, layer):
    bsz, n, w = hf.shape
    tm = ROW_TILE
    tile = pl.BlockSpec((1, tm, w), lambda b, t: (b, t, 0))
    return pl.pallas_call(
        _ya_kernel, grid=(bsz, n // tm),
        in_specs=[tile, tile, pl.BlockSpec((1, tm, w), lambda b, t: (b, t, COL_OA // w)),
                  pl.BlockSpec((None, 1, w), lambda b, t: (layer, 0, 0))],
        out_specs=tile, out_shape=jax.ShapeDtypeStruct((bsz, n, w), BF16),
        compiler_params=_params(("arbitrary", "arbitrary"), 40),
        name="mlstm_out",
    )(hf, hb, proj, norm_g.reshape(norm_g.shape[0], 1, w))


def _s5_kernel(uf_ref, ub_ref, bm_ref, cm_ref, a_ref, yf_ref, yb_ref, lhs_ref, re_ref, im_ref, y_ref, carry_ref):
    nb, t, _ = uf_ref.shape
    rows = t * nb
    sw = S5_BUNDLE * S5_STATE
    tiles_per_bundle = sw // LANES
    in_slabs = MIX_W // LANES

    @pl.when(pl.program_id(0) == 0)
    def _():
        carry_ref[...] = jnp.zeros_like(carry_ref)

    kb_per_half = S5_NB // S5_HALVES

    def interleave_inputs(d, u_ref):
        for b in range(nb):
            for s in range(in_slabs):
                lhs_ref[d, s, pl.ds(b, t, stride=nb), :] = u_ref[b, :, s * LANES:(s + 1) * LANES]

    def b_proj(d, kb):
        lhs = jnp.concatenate([lhs_ref[d, 2 * kb], lhs_ref[d, 2 * kb + 1]], axis=1).astype(BF16)
        bu = _dot(lhs, bm_ref[d, kb])
        half = kb // kb_per_half
        for lt in range(tiles_per_bundle):
            s = (kb % kb_per_half) * tiles_per_bundle + lt
            re_ref[d, s, pl.ds(half, rows, stride=S5_HALVES), :] = bu[:, lt * LANES:(lt + 1) * LANES]
            im_ref[d, s, pl.ds(half, rows, stride=S5_HALVES), :] = bu[:, sw + lt * LANES:sw + (lt + 1) * LANES]

    def c_proj(d, kb):
        half = kb // kb_per_half
        s0 = (kb % kb_per_half) * tiles_per_bundle
        sr = jnp.concatenate([re_ref[d, s0 + lt, pl.ds(half, rows, stride=S5_HALVES), :]
                              for lt in range(tiles_per_bundle)], axis=1).astype(BF16)
        si = jnp.concatenate([im_ref[d, s0 + lt, pl.ds(half, rows, stride=S5_HALVES), :]
                              for lt in range(tiles_per_bundle)], axis=1).astype(BF16)
        y = _dot(sr, cm_ref[kb, 0]) + _dot(si, cm_ref[kb, 1])
        y_ref[d, 2 * kb] = y[:, :LANES]
        y_ref[d, 2 * kb + 1] = y[:, LANES:]

    def scan_group(g, side_work):
        slabs = tuple(range(g * S5_GRP, (g + 1) * S5_GRP))
        xs = [carry_ref[d, c, s] for d in range(2) for s in slabs for c in range(2)]
        every = t // max(len(side_work), 1)
        for i in range(t):
            if side_work and i % every == 0 and i // every < len(side_work):
                side_work[i // every]()
            for d in range(2):
                r0 = (i if d == 0 else t - 1 - i) * SUBLANES
                for j, s in enumerate(slabs):
                    k = 2 * (d * S5_GRP + j)
                    xr, xi = xs[k], xs[k + 1]
                    ar, ai = a_ref[d, 0, s], a_ref[d, 1, s]
                    nr = ar * xr - ai * xi + re_ref[d, s, r0:r0 + SUBLANES, :]
                    ni = ar * xi + ai * xr + im_ref[d, s, r0:r0 + SUBLANES, :]
                    re_ref[d, s, r0:r0 + SUBLANES, :] = nr
                    im_ref[d, s, r0:r0 + SUBLANES, :] = ni
                    xs[k], xs[k + 1] = nr, ni
        for d in range(2):
            for j, s in enumerate(slabs):
                for c in range(2):
                    carry_ref[d, c, s] = xs[2 * (d * S5_GRP + j) + c]

    assert S5_SLABS // S5_GRP == kb_per_half == 2
    bundles = lambda g: [kb for kb in range(S5_NB) if kb % kb_per_half == g]
    for d, u_ref in enumerate((uf_ref, ub_ref)):
        interleave_inputs(d, u_ref)
        for kb in bundles(0):
            b_proj(d, kb)
    scan_group(0, [functools.partial(b_proj, d, kb) for d in range(2) for kb in bundles(1)])
    scan_group(1, [functools.partial(c_proj, d, kb) for d in range(2) for kb in bundles(0)])
    for d, out_ref in enumerate((yf_ref, yb_ref)):
        for kb in bundles(1):
            c_proj(d, kb)
        for b in range(nb):
            for s in range(in_slabs):
                out_ref[b, :, s * LANES:(s + 1) * LANES] = y_ref[d, s, pl.ds(b, t, stride=nb), :]


def s5_scan(proj, bmat, cmat, atab, n_ctx):
    bsz, n, _ = proj.shape
    assert bsz * S5_HALVES == SUBLANES, "the scan layout puts (sample, state half) on the 8 sublanes"
    t = S5_TC
    nc = n // t
    c0 = n_ctx // t
    w = MIX_W
    rows = t * bsz
    bwd = lambda j: _rev_chunk(j, c0, nc)
    out = jax.ShapeDtypeStruct((bsz, n, w), F32)
    const = lambda a: pl.BlockSpec(a.shape, lambda j: (0,) * a.ndim)
    return pl.pallas_call(
        _s5_kernel, grid=(nc,),
        in_specs=[pl.BlockSpec((bsz, t, w), lambda j: (0, j, COL_US // w)),
                  pl.BlockSpec((bsz, t, w), lambda j: (0, bwd(j), COL_US // w)),
                  const(bmat), const(cmat), const(atab)],
        out_specs=[pl.BlockSpec((bsz, t, w), lambda j: (0, j, 0)),
                   pl.BlockSpec((bsz, t, w), lambda j: (0, bwd(j), 0))],
        out_shape=[out, out],
        scratch_shapes=[pltpu.VMEM((2, w // LANES, rows, LANES), F32),
                        pltpu.VMEM((2, S5_SLABS, rows * S5_HALVES, LANES), F32),
                        pltpu.VMEM((2, S5_SLABS, rows * S5_HALVES, LANES), F32),
                        pltpu.VMEM((2, w // LANES, rows, LANES), F32),
                        pltpu.VMEM((2, 2, S5_SLABS, SUBLANES, LANES), F32)],
        compiler_params=_params(("arbitrary",), 56),
        name="s5_scan",
    )(proj, proj, bmat, cmat, atab)


def s5_operands(lam_re, lam_im, log_dt, b_re, b_im, c_re, c_im):
    lam_re, lam_im = lam_re.astype(F32), lam_im.astype(F32)
    dt = jnp.exp(log_dt.astype(F32))[..., None]
    mag = jnp.exp(lam_re * dt)
    a_re, a_im = mag * jnp.cos(lam_im * dt), mag * jnp.sin(lam_im * dt)
    nr, ni = a_re - 1.0, a_im
    den = lam_re * lam_re + lam_im * lam_im
    fr = (nr * lam_re + ni * lam_im) / den
    fi = (ni * lam_re - nr * lam_im) / den
    bb_re = fr[..., None] * b_re - fi[..., None] * b_im
    bb_im = fr[..., None] * b_im + fi[..., None] * b_re
    bw, sw = S5_BUNDLE * S5_GROUP, S5_BUNDLE * S5_STATE

    def blockdiag_b(bb):
        rows = bb.reshape(2, S5_NB, S5_BUNDLE, S5_STATE, S5_GROUP).transpose(0, 1, 2, 4, 3).reshape(
            2, S5_NB, bw, S5_STATE)
        on_diag = (jnp.arange(bw)[:, None] // S5_GROUP) == (jnp.arange(sw)[None, :] // S5_STATE)
        return jnp.where(on_diag, jnp.tile(rows, (1, 1, 1, S5_BUNDLE)), 0.0)

    bmat = jnp.concatenate([blockdiag_b(bb_re), blockdiag_b(bb_im)], -1).astype(BF16)

    def blockdiag_c(c):
        rows = c.astype(F32).reshape(S5_NB, S5_BUNDLE, S5_GROUP, S5_STATE).transpose(0, 1, 3, 2).reshape(
            S5_NB, sw, S5_GROUP)
        on_diag = (jnp.arange(sw)[:, None] // S5_STATE) == (jnp.arange(bw)[None, :] // S5_GROUP)
        return jnp.where(on_diag, jnp.tile(rows, (1, 1, S5_BUNDLE)), 0.0)

    cmat = jnp.stack([blockdiag_c(c_re), -blockdiag_c(c_im)], 1).astype(BF16)

    def slab_table(a):
        a4 = a.reshape(2, S5_HALVES, S5_SLABS, LANES)
        return jnp.take(a4, jnp.arange(SUBLANES) % S5_HALVES, axis=1).transpose(0, 2, 1, 3)

    atab = jnp.stack([slab_table(a_re), slab_table(a_im)], 1)
    return bmat, cmat, atab


def _glu_kernel(yf_ref, yb_ref, u_ref, d_ref, w_ref, b_ref, o_ref):
    y = d_ref[...] * u_ref[0] + yf_ref[0] + yb_ref[0]
    z = jax.nn.gelu(y)
    o_ref[0] = (z * jax.nn.sigmoid(_dot(z.astype(BF16), w_ref[...].astype(BF16)) + b_ref[...])).astype(o_ref.dtype)


def s5_glu(yf, yb, proj, d_skip, glu_w, glu_b, layer):
    bsz, n, w = yf.shape
    tm = ROW_TILE
    tile = pl.BlockSpec((1, tm, w), lambda b, t: (b, t, 0))
    vec = pl.BlockSpec((None, 1, w), lambda b, t: (layer, 0, 0))
    return pl.pallas_call(
        _glu_kernel, grid=(bsz, n // tm),
        in_specs=[tile, tile, pl.BlockSpec((1, tm, w), lambda b, t: (b, t, COL_US // w)),
                  vec, pl.BlockSpec((None, w, w), lambda b, t: (layer, 0, 0)), vec],
        out_specs=tile, out_shape=jax.ShapeDtypeStruct((bsz, n, w), BF16),
        compiler_params=_params(("arbitrary", "arbitrary"), 40),
        name="s5_glu",
    )(yf, yb, proj, d_skip.reshape(-1, 1, w), glu_w, glu_b.reshape(-1, 1, w))


def _da_prep_kernel(x_ref, g_ref, cos_ref, sin_ref, o_ref):
    cos = cos_ref[...]
    sin = sin_ref[...]
    lane = lax.broadcasted_iota(jnp.int32, cos.shape, 1)
    first = lane < DA_DQK
    quarter = DA_DQK // 4
    up = (lane & quarter) == 0
    scale = jnp.where(pl.program_id(2) == 0, DA_DQK ** -0.5 * math.log2(math.e), 1.0)
    slices = [slice(hd * LANES, (hd + 1) * LANES) for hd in range(DA_HEADS)]
    xs = [x_ref[0, :, sl] for sl in slices]
    sq = [x * x for x in xs]
    ms0 = [jnp.sum(jnp.where(first, s, 0.0), axis=-1, keepdims=True) for s in sq]
    ms1 = [jnp.sum(jnp.where(first, 0.0, s), axis=-1, keepdims=True) for s in sq]
    ys = [x * lax.rsqrt(jnp.where(first, a, b) * (1.0 / DA_DQK) + EPS) * g_ref[:, sl]
          for x, a, b, sl in zip(xs, ms0, ms1, slices)]
    partners = [jnp.where(up, pltpu.roll(y, LANES - quarter, 1), pltpu.roll(y, quarter, 1)) for y in ys]
    for y, p, sl in zip(ys, partners, slices):
        o_ref[0, :, sl] = ((y * cos + p * sin) * scale).astype(o_ref.dtype)


def da_prep(proj, qn_g, kn_g, cos_t, sin_t, layer):
    bsz, n, _ = proj.shape
    tm = ROW_TILE
    w = DA_HEADS * LANES
    reps = w // DA_DQK
    gains = jnp.concatenate([jnp.tile(qn_g, (1, reps)), jnp.tile(kn_g, (1, reps))], -1)
    gains = gains.reshape(gains.shape[0], 1, 2 * w)
    return pl.pallas_call(
        _da_prep_kernel, grid=(bsz, n // tm, 2),
        in_specs=[pl.BlockSpec((1, tm, w), lambda b, t, j: (b, t, COL_QD // w + j)),
                  pl.BlockSpec((None, 1, w), lambda b, t, j: (layer, 0, j)),
                  pl.BlockSpec((tm, LANES), lambda b, t, j: (t, 0)),
                  pl.BlockSpec((tm, LANES), lambda b, t, j: (t, 0))],
        out_specs=pl.BlockSpec((1, tm, w), lambda b, t, j: (b, t, j)),
        out_shape=jax.ShapeDtypeStruct((bsz, n, 2 * w), BF16),
        compiler_params=_params(("arbitrary", "arbitrary", "arbitrary"), 40),
        name="da_prep",
    )(proj, gains, cos_t, sin_t)


def rope_tables(n, n_ctx):
    lat = jnp.arange(n) - n_ctx
    rows = (lat // GRID_W).astype(F32)
    cols = (lat % GRID_W).astype(F32)
    lane = jnp.arange(LANES)
    in_comp = lane % DA_DQK
    quarter = DA_DQK // 4
    freqs = ROPE_BASE ** (-(in_comp % quarter).astype(F32) / quarter)
    pos = jnp.where((in_comp < DA_DQK // 2)[None, :], rows[:, None], cols[:, None])
    ang = pos * freqs[None, :]
    sign = jnp.where((in_comp % (2 * quarter)) < quarter, -1.0, 1.0)
    is_lat = (lat >= 0)[:, None]
    cos_t = jnp.where(is_lat, jnp.cos(ang), 1.0)
    sin_t = jnp.where(is_lat, jnp.sin(ang) * sign[None, :], 0.0)
    return cos_t.astype(F32), sin_t.astype(F32)


DA_HEADS_PER_STEP = 4


def _diff_attn_kernel(scal_ref, q_ref, k_ref, v_ref, g_ref, o_ref, *, n_ctx, ctx_tiles):
    lam = scal_ref[0]
    out_scale = scal_ref[1]
    lane = lax.broadcasted_iota(jnp.int32, (q_ref.shape[1], LANES), 1)
    first = lane < DA_DQK

    def exp_rows(s):
        e = jnp.exp2(s - jnp.max(s, axis=-1, keepdims=True))
        return e, jnp.sum(e, axis=-1, keepdims=True)

    def attend(nk):
        for hh in range(DA_HEADS_PER_STEP):
            sl = slice(hh * LANES, (hh + 1) * LANES)
            q = q_ref[0, :, sl]
            zero = jnp.zeros_like(q)
            k = k_ref[0, :nk, sl]
            e0, l0 = exp_rows(_dot_nt(jnp.where(first, q, zero), k))
            e1, l1 = exp_rows(_dot_nt(jnp.where(first, zero, q), k))
            a = e0 - (lam * l0 * (1.0 / l1)) * e1
            o = _dot(a.astype(BF16), v_ref[0, :nk, sl].astype(BF16)) * (1.0 / l0)
            o = o * lax.rsqrt(jnp.mean(o * o, axis=-1, keepdims=True) + EPS) * g_ref[...]
            o_ref[0, :, sl] = (o * out_scale).astype(o_ref.dtype)

    @pl.when(pl.program_id(2) < ctx_tiles)
    def _():
        attend(n_ctx)

    @pl.when(pl.program_id(2) >= ctx_tiles)
    def _():
        attend(k_ref.shape[1])


def diff_attn(qk_hat, proj, scal, sub_g, layer, n_ctx):
    bsz, n, _ = qk_hat.shape
    tq = SEQ_CHUNK
    w = DA_HEADS_PER_STEP * LANES
    return pl.pallas_call(
        functools.partial(_diff_attn_kernel, n_ctx=n_ctx, ctx_tiles=n_ctx // tq),
        grid=(bsz, DA_HEADS // DA_HEADS_PER_STEP, n // tq),
        in_specs=[pl.BlockSpec(memory_space=pltpu.SMEM),
                  pl.BlockSpec((1, tq, w), lambda b, h, t: (b, t, h)),
                  pl.BlockSpec((1, n, w), lambda b, h, t: (b, 0, DA_HEADS // DA_HEADS_PER_STEP + h)),
                  pl.BlockSpec((1, n, w), lambda b, h, t: (b, 0, COL_VD // w + h)),
                  pl.BlockSpec((None, 1, LANES), lambda b, h, t: (layer, 0, 0))],
        out_specs=pl.BlockSpec((1, tq, w), lambda b, h, t: (b, t, h)),
        out_shape=jax.ShapeDtypeStruct((bsz, n, DA_HEADS * DA_DV), BF16),
        compiler_params=_params(("arbitrary", "arbitrary", "arbitrary"), 48),
        name="diff_attn",
    )(scal, qk_hat, qk_hat, proj, sub_g.reshape(sub_g.shape[0], 1, DA_DV))


def _merge_kernel(ya_ref, yb_ref, yc_ref, p_ref, ga_ref, gb_ref, gc_ref, o_ref):
    acc = _dot(ya_ref[0], p_ref[0].astype(BF16)) * jax.nn.sigmoid(ga_ref[0])
    acc += _dot(yb_ref[0], p_ref[1].astype(BF16)) * jax.nn.sigmoid(gb_ref[0])
    acc += _dot(yc_ref[0], p_ref[2].astype(BF16)) * jax.nn.sigmoid(gc_ref[0])
    o_ref[0] = acc.astype(o_ref.dtype)


def branch_merge(ya, yb, yc, proj, branch_proj, layer):
    bsz, n, w = ya.shape
    d = branch_proj.shape[-1]
    tm, tn = 768, 512
    y_tile = pl.BlockSpec((1, tm, w), lambda j, b, t: (b, t, 0))
    gate = lambda r: pl.BlockSpec((1, tm, tn), lambda j, b, t: (b, t, (COL_GB + r * d) // tn + j))
    return pl.pallas_call(
        _merge_kernel, grid=(d // tn, bsz, n // tm),
        in_specs=[y_tile, y_tile, y_tile,
                  pl.BlockSpec((None, N_BRANCH, w, tn), lambda j, b, t: (layer, 0, 0, j)),
                  gate(0), gate(1), gate(2)],
        out_specs=pl.BlockSpec((1, tm, tn), lambda j, b, t: (b, t, j)),
        out_shape=jax.ShapeDtypeStruct((bsz, n, d), BF16),
        compiler_params=_params(("arbitrary", "arbitrary", "arbitrary"), 48),
        name="branch_merge",
    )(ya, yb, yc, branch_proj, proj, proj, proj)


INFO_E, INFO_POS, INFO_W = 0, 2, 4


def _pack_bf16_pair(x):
    w = x.shape[1] // 2
    lo = lax.bitcast_convert_type(x[:, :w].astype(BF16).astype(F32), jnp.uint32)
    hi = lax.bitcast_convert_type(x[:, w:].astype(BF16).astype(F32), jnp.uint32)
    return (lo >> 16) | (hi & jnp.uint32(0xFFFF0000))


def _unpack_bf16_pair(p):
    lo = lax.bitcast_convert_type(p << 16, F32)
    hi = lax.bitcast_convert_type(p & jnp.uint32(0xFFFF0000), F32)
    return jnp.concatenate([lo, hi], axis=1)


def _router_kernel(z_ref, g_ref, shift_ref, scale_ref, w_ref, b_ref, hp_ref, info_ref, cnt_ref):
    tm = z_ref.shape[1]

    @pl.when((pl.program_id(0) == 0) & (pl.program_id(1) == 0))
    def _():
        cnt_ref[...] = jnp.zeros_like(cnt_ref)

    z = z_ref[0]
    h = z * lax.rsqrt(jnp.mean(z * z, axis=-1, keepdims=True) + EPS) * g_ref[...]
    h = h * (1.0 + scale_ref[...]) + shift_ref[...]
    hp_ref[0] = _pack_bf16_pair(h)
    h_parts = _split3(h)
    w_parts = _split3(w_ref[...])
    logits = (_dot_nt(w_parts[0], h_parts[0]) + _dot_nt(w_parts[0], h_parts[1])
              + _dot_nt(w_parts[1], h_parts[0]))
    score = jax.nn.sigmoid(logits)
    sel = score + b_ref[...]
    rows = [sel[e:e + 1, :] for e in range(N_EXPERTS)]
    gscores = []
    for g in range(N_EXPERT_GROUPS):
        a, b, c, d = rows[g * EXPERTS_PER_GROUP:(g + 1) * EXPERTS_PER_GROUP]
        hi1, lo1 = jnp.maximum(a, b), jnp.minimum(a, b)
        hi2, lo2 = jnp.maximum(c, d), jnp.minimum(c, d)
        gscores.append(jnp.maximum(hi1, hi2) + jnp.maximum(jnp.minimum(hi1, hi2), jnp.maximum(lo1, lo2)))
    gmax = functools.reduce(jnp.maximum, gscores)
    taken = jnp.zeros_like(gmax, dtype=jnp.bool_)
    picked = []
    for g in range(N_EXPERT_GROUPS):
        is_g = (gscores[g] == gmax) & jnp.logical_not(taken)
        picked.append(is_g)
        taken = taken | is_g
    chosen = []
    for e in range(N_EXPERTS):
        g = e // EXPERTS_PER_GROUP
        rank = jnp.zeros_like(gmax)
        for o in range(g * EXPERTS_PER_GROUP, (g + 1) * EXPERTS_PER_GROUP):
            if o == e:
                continue
            ahead = (rows[o] > rows[e]) | ((rows[o] == rows[e]) & (o < e))
            rank = rank + ahead.astype(F32)
        chosen.append(picked[g] & (rank < float(TOP_K)))
    onehot = jnp.concatenate([c.astype(F32) for c in chosen], axis=0)
    total = jnp.sum(onehot * score, axis=0, keepdims=True)
    gates = onehot * score * (1.0 / total)

    r_i = lax.broadcasted_iota(jnp.int32, (tm, tm), 0)
    c_i = lax.broadcasted_iota(jnp.int32, (tm, tm), 1)
    incl = _dot(onehot.astype(BF16), (r_i <= c_i).astype(BF16))
    pos = incl - onehot + cnt_ref[:, :1]
    cnt_ref[...] = cnt_ref[...] + incl[:, tm - 1:tm]

    seen = jnp.zeros_like(gmax)
    rec = [jnp.zeros_like(gmax) for _ in range(3 * TOP_K)]
    for e in range(N_EXPERTS):
        oh = onehot[e:e + 1, :]
        for k, is_k in enumerate((oh * (1.0 - seen), oh * seen)):
            rec[INFO_E + k] += is_k * float(e)
            rec[INFO_POS + k] += is_k * pos[e:e + 1, :]
            rec[INFO_W + k] += is_k * gates[e:e + 1, :]
        seen = seen + oh
    pad = jnp.zeros((LANES - len(rec), tm), F32)
    info_ref[0] = jnp.concatenate(rec + [pad], axis=0).T


def moe_router(z, norm_g, mods, layer, n_ctx, router_w, router_b):
    bsz, n, d = z.shape
    tm = SEQ_CHUNK
    ctx_tiles = n_ctx // tm
    return pl.pallas_call(
        _router_kernel, grid=(bsz, n // tm),
        in_specs=[pl.BlockSpec((1, tm, d), lambda b, t: (b, t, 0)),
                  pl.BlockSpec((None, 1, d), lambda b, t: (layer, 0, 0)),
                  _mod_spec(layer, 3, bsz, ctx_tiles), _mod_spec(layer, 4, bsz, ctx_tiles),
                  pl.BlockSpec((N_EXPERTS, d), lambda b, t: (0, 0)),
                  pl.BlockSpec((N_EXPERTS, 1), lambda b, t: (0, 0))],
        out_specs=[pl.BlockSpec((1, tm, d // 2), lambda b, t: (b, t, 0)),
                   pl.BlockSpec((1, tm, LANES), lambda b, t: (b, t, 0)),
                   pl.BlockSpec((N_EXPERTS, LANES), lambda b, t: (0, 0))],
        out_shape=[jax.ShapeDtypeStruct((bsz, n, d // 2), jnp.uint32),
                   jax.ShapeDtypeStruct((bsz, n, LANES), F32),
                   jax.ShapeDtypeStruct((N_EXPERTS, LANES), F32)],
        compiler_params=_params(("arbitrary", "arbitrary"), 40),
        name="moe_router",
    )(z, norm_g.reshape(norm_g.shape[0], 1, d), mods, mods, router_w.T, router_b.reshape(N_EXPERTS, 1))


def moe_plan(info, counts, n_tiles):
    m = info.shape[0] * info.shape[1]
    rec = info.reshape(m, LANES)
    cnt = counts[:, 0].astype(jnp.int32)
    padded = (cnt + MOE_TILE - 1) // MOE_TILE * MOE_TILE
    ends = jnp.cumsum(padded)
    starts = ends - padded
    experts = rec[:, INFO_E:INFO_E + TOP_K].astype(jnp.int32)
    dest = starts[experts] + rec[:, INFO_POS:INFO_POS + TOP_K].astype(jnp.int32)
    tile_start = jnp.arange(n_tiles, dtype=jnp.int32) * MOE_TILE
    tile_expert = jnp.minimum(jnp.sum(ends[None, :] <= tile_start[:, None], axis=1), N_EXPERTS - 1).astype(jnp.int32)
    n_used = (ends[-1] // MOE_TILE).astype(jnp.int32).reshape(1)
    return dest.reshape(-1), tile_expert, n_used


def _row_copy(src_ref, src_row, dst_ref, dst_row, sem):
    return pltpu.make_async_copy(src_ref.at[pl.ds(src_row, 1), :], dst_ref.at[pl.ds(dst_row, 1), :], sem)


def _dispatch_kernel(dest_ref, h_ref, xs_in_ref, xs_ref, sem):
    del xs_in_ref
    tm = h_ref.shape[0]
    base = pl.program_id(0) * tm

    def start(r, carry):
        for k in range(TOP_K):
            _row_copy(h_ref, r, xs_ref, dest_ref[(base + r) * TOP_K + k], sem).start(priority=k)
        return carry

    def wait(r, carry):
        for k in range(TOP_K):
            _row_copy(h_ref, 0, xs_ref, 0, sem).wait()
        return carry

    lax.fori_loop(0, tm, start, 0, unroll=DMA_UNROLL)
    lax.fori_loop(0, tm, wait, 0, unroll=DMA_UNROLL)


def moe_dispatch(h2, dest, xs_init):
    m, d = h2.shape
    n_rows = xs_init.shape[0]
    tm = SEQ_CHUNK
    grid_spec = pltpu.PrefetchScalarGridSpec(
        num_scalar_prefetch=1, grid=(m // tm,),
        in_specs=[pl.BlockSpec((tm, d), lambda i, dest: (i, 0)),
                  pl.BlockSpec(memory_space=pl.ANY)],
        out_specs=pl.BlockSpec(memory_space=pl.ANY),
        scratch_shapes=[pltpu.SemaphoreType.DMA])
    return pl.pallas_call(
        _dispatch_kernel, grid_spec=grid_spec,
        out_shape=jax.ShapeDtypeStruct((n_rows, d), h2.dtype),
        input_output_aliases={2: 0},
        compiler_params=_params(("arbitrary",), 40),
        name="moe_dispatch",
    )(dest, h2, xs_init)


def _expert_kernel(te_ref, nu_ref, x_ref, wg_ref, wu_ref, wd_ref, o_ref, wg_scr, wu_scr, wd_scr):
    i = pl.program_id(0)
    used = i < nu_ref[0]

    @pl.when(used & ((i == 0) | (te_ref[i] != te_ref[jnp.maximum(i - 1, 0)])))
    def _():
        wg_scr[...] = wg_ref[...].astype(BF16)
        wu_scr[...] = wu_ref[...].astype(BF16)
        wd_scr[...] = wd_ref[...].astype(BF16)

    @pl.when(used)
    def _():
        x = _unpack_bf16_pair(x_ref[...]).astype(BF16)
        a = _dot(x, wg_scr[...])
        act = (a * jax.nn.sigmoid(a)) * _dot(x, wu_scr[...])
        o_ref[...] = _pack_bf16_pair(_dot(act.astype(BF16), wd_scr[...]))

    @pl.when(jnp.logical_not(used))
    def _():
        o_ref[...] = jnp.zeros_like(o_ref)


def moe_experts(xs, tile_expert, n_used, w_gate, w_up, w_down, layer):
    n_rows, dp = xs.shape
    d, f = w_gate.shape[-2:]
    w_spec = lambda r, c: pl.BlockSpec((None, None, r, c), lambda i, te, nu: (layer, te[i], 0, 0))
    grid_spec = pltpu.PrefetchScalarGridSpec(
        num_scalar_prefetch=2, grid=(n_rows // MOE_TILE,),
        in_specs=[pl.BlockSpec((MOE_TILE, dp), lambda i, te, nu: (i, 0)),
                  w_spec(d, f), w_spec(d, f), w_spec(f, d)],
        out_specs=pl.BlockSpec((MOE_TILE, dp), lambda i, te, nu: (i, 0)),
        scratch_shapes=[pltpu.VMEM((d, f), BF16), pltpu.VMEM((d, f), BF16), pltpu.VMEM((f, d), BF16)])
    return pl.pallas_call(
        _expert_kernel, grid_spec=grid_spec,
        out_shape=jax.ShapeDtypeStruct((n_rows, dp), jnp.uint32),
        compiler_params=_params(("arbitrary",), 56),
        name="moe_experts",
    )(tile_expert, n_used, xs, w_gate, w_up, w_down)


def _combine_kernel(dest_ref, ys_ref, info_ref, z_ref, gate_ref, *rest, with_norm):
    if with_norm:
        g_ref, shift_ref, scale_ref, o_ref, h_ref, buf_ref, sem = rest
    else:
        o_ref, buf_ref, sem = rest
    tm = z_ref.shape[1]
    base = (pl.program_id(0) * pl.num_programs(1) + pl.program_id(1)) * tm

    def start(r, carry):
        for k in range(TOP_K):
            _row_copy(ys_ref, dest_ref[(base + r) * TOP_K + k], buf_ref.at[k], r, sem).start(priority=k)
        return carry

    def wait(r, carry):
        for k in range(TOP_K):
            _row_copy(ys_ref, 0, buf_ref.at[k], 0, sem).wait()
        return carry

    lax.fori_loop(0, tm, start, 0, unroll=DMA_UNROLL)
    lax.fori_loop(0, tm, wait, 0, unroll=DMA_UNROLL)
    info = info_ref[0]
    mix = (info[:, INFO_W:INFO_W + 1] * _unpack_bf16_pair(buf_ref[0])
           + info[:, INFO_W + 1:INFO_W + 2] * _unpack_bf16_pair(buf_ref[1]))
    z = z_ref[0] + gate_ref[...] * mix
    o_ref[0] = z
    if with_norm:
        y = z * lax.rsqrt(jnp.mean(z * z, axis=-1, keepdims=True) + EPS) * g_ref[...]
        h_ref[0] = (y * (1.0 + scale_ref[...]) + shift_ref[...]).astype(h_ref.dtype)


def moe_combine(ys, dest, info, z, mods, layer, gate_chunk, n_ctx, next_norm_g=None):
    bsz, n, d = z.shape
    tm = SEQ_CHUNK
    ctx_tiles = n_ctx // tm
    with_norm = next_norm_g is not None
    tile = pl.BlockSpec((1, tm, d), lambda b, t, dest: (b, t, 0))
    mod_row = lambda lyr, chunk: pl.BlockSpec(
        (None, None, 1, d), lambda b, t, dest: (lyr, jnp.where(t < ctx_tiles, bsz, b), 0, chunk))
    in_specs = [pl.BlockSpec(memory_space=pl.ANY),
                pl.BlockSpec((1, tm, LANES), lambda b, t, dest: (b, t, 0)),
                tile, mod_row(layer, gate_chunk)]
    args = [dest, ys, info, z, mods]
    out_specs, out_shape = tile, jax.ShapeDtypeStruct(z.shape, F32)
    if with_norm:
        in_specs += [pl.BlockSpec((None, 1, d), lambda b, t, dest: (layer + 1, 0, 0)),
                     mod_row(layer + 1, 0), mod_row(layer + 1, 1)]
        args += [next_norm_g.reshape(next_norm_g.shape[0], 1, d), mods, mods]
        out_specs, out_shape = [tile, tile], [out_shape, jax.ShapeDtypeStruct(z.shape, BF16)]
    grid_spec = pltpu.PrefetchScalarGridSpec(
        num_scalar_prefetch=1, grid=(bsz, n // tm), in_specs=in_specs, out_specs=out_specs,
        scratch_shapes=[pltpu.VMEM((TOP_K, tm, ys.shape[1]), ys.dtype), pltpu.SemaphoreType.DMA])
    return pl.pallas_call(
        functools.partial(_combine_kernel, with_norm=with_norm), grid_spec=grid_spec, out_shape=out_shape,
        compiler_params=_params(("arbitrary", "arbitrary"), 40),
        name="moe_combine",
    )(*args)


def kernel(x, c, ctx, c_ctx, norm1_g, norm2_g, ada_w, ada_b, w_in, b_in, ml_conv_w, ml_conv_b, ml_norm_g, s5_lam_re, s5_lam_im, s5_log_dt, s5_b_re, s5_b_im, s5_c_re, s5_c_im, s5_d, s5_glu_w, s5_glu_b, da_q_norm_g, da_k_norm_g, da_lam_q1, da_lam_k1, da_lam_q2, da_lam_k2, da_sub_norm_g, branch_proj, w_out, router_w, router_b, exp_w_gate, exp_w_up, exp_w_down):
    bsz, n_lat, d = x.shape
    n_ctx = ctx.shape[1]
    n = n_ctx + n_lat
    depth = w_in.shape[0]
    assert n_ctx % SEQ_CHUNK == 0 and n_lat % SEQ_CHUNK == 0 and d == D_MODEL

    rows = -(-(bsz + 1) // SUBLANES) * SUBLANES
    cvec = jnp.zeros((rows, d), F32).at[:bsz].set(c).at[bsz].set(c_ctx)
    mods = ada_mod(cvec, ada_w, ada_b).reshape(depth, rows, 1, 6 * d)

    w_in_t = jnp.swapaxes(w_in, 1, 2)
    b_in3 = b_in[:, None, :]
    b_ml = b_in3[..., :GATE_OFF]
    b_rest = b_in3[..., REST_OFF:]
    m_tok = bsz * n
    moe_tiles = m_tok * TOP_K // MOE_TILE + N_EXPERTS
    xs = jnp.zeros((moe_tiles * MOE_TILE, d // 2), jnp.uint32)
    cos_t, sin_t = rope_tables(n, n_ctx)
    z = jnp.concatenate([ctx, x], 1)
    (h,) = norm_mod(z, norm1_g, mods, 0, 0, 1, n_ctx, (BF16,))
    for i in range(depth):
        lam_init = 0.8 - 0.6 * math.exp(-0.3 * i)
        proj_ml = in_proj(h, w_in_t, b_ml, i, 0)
        proj = in_proj(h, w_in_t, b_rest, i, REST_OFF)
        gates = gate_proj(h, w_in_t, b_in3, i, GATE_OFF)
        gates_t = jnp.swapaxes(gates[..., :N_GATES], 1, 2)

        qk, qkt = conv_silu(proj_ml, ml_conv_w, ml_conv_b, i, n_ctx)
        hf, hb = mlstm(qk, qkt, proj_ml, gates, gates_t, n_ctx)
        ya = mlstm_out(hf, hb, proj_ml, ml_norm_g, i)

        bmat, cmat, atab = s5_operands(s5_lam_re[i], s5_lam_im[i], s5_log_dt[i], s5_b_re[i], s5_b_im[i],
                                       s5_c_re[i], s5_c_im[i])
        yf, ybk = s5_scan(proj, bmat, cmat, atab, n_ctx)
        yb = s5_glu(yf, ybk, proj, s5_d, s5_glu_w, s5_glu_b, i)

        qk_hat = da_prep(proj, da_q_norm_g, da_k_norm_g, cos_t, sin_t, i)
        lam = (jnp.exp(jnp.sum(da_lam_q1[i].astype(F32) * da_lam_k1[i].astype(F32)))
               - jnp.exp(jnp.sum(da_lam_q2[i].astype(F32) * da_lam_k2[i].astype(F32))) + lam_init)
        scal = jnp.stack([lam, jnp.asarray(1.0 - lam_init, F32)]).astype(F32)
        yc = diff_attn(qk_hat, proj, scal, da_sub_norm_g, i, n_ctx)

        merged = branch_merge(ya, yb, yc, proj, branch_proj, i)
        z = matmul_resid(merged, w_out, z, mods, i, 2, n_ctx)

        h2p, info, counts = moe_router(z, norm2_g, mods, i, n_ctx, router_w, router_b)
        dest, tile_expert, n_used = moe_plan(info, counts, moe_tiles)
        xs = moe_dispatch(h2p.reshape(m_tok, d // 2), dest, xs)
        ys = moe_experts(xs, tile_expert, n_used, exp_w_gate, exp_w_up, exp_w_down, i)
        if i + 1 < depth:
            z, h = moe_combine(ys, dest, info, z, mods, i, 5, n_ctx, next_norm_g=norm1_g)
        else:
            z = moe_combine(ys, dest, info, z, mods, i, 5, n_ctx)
    return z[:, n_ctx:]
```
